```python
import math
import jax, jax.numpy as jnp
from jax import lax
import numpy as np


D_MODEL = 1024
BATCH = 4
SEQ = 4096
DEPTH = 1
DEC_BATCH = 32
DEC_SEQ = 1
PAST_LEN = 8192
PAGE_SIZE = 128

N_MEM = 256
SB_HEADS = 8
SB_DH = 64
SB_W = SB_HEADS * SB_DH
SB_BIAS_INIT = -7.5
HG_HEADS = 4
HG_DK = 64
HG_DV = 64
HG_WK = HG_HEADS * HG_DK
HG_WV = HG_HEADS * HG_DV
XA_HEADS = 4
XA_DH = 64
XA_W = XA_HEADS * XA_DH
D_MIX = SB_W + HG_WV + XA_W
D_IN = 4 * SB_W + 2 * HG_WK + 2 * HG_WV + 2 * XA_W
Q_BLOCK = 128
HG_CHUNK = 64
EPS = 1e-6
F32 = jnp.float32

kernel_name = 'hymba_stickbreak_hgrn2_memxattn_step'


def _rmsnorm(x, g):
    xf = x.astype(F32)
    y = xf * lax.rsqrt(jnp.mean(xf * xf, axis=-1, keepdims=True) + EPS)
    return (y * g.astype(F32)).astype(x.dtype)


def _mixer_inputs(x, g_norm, w_in, lb):
    b, t, _ = x.shape
    h = _rmsnorm(x, g_norm) @ w_in
    sizes = (SB_W, SB_W, SB_W, SB_W, HG_WK, HG_WK, HG_WV, HG_WV, XA_W, XA_W)
    idx = []
    acc = 0
    for s in sizes[:-1]:
        acc += s
        idx.append(acc)
    sb_q, sb_k, sb_v, sb_g, hg_q, hg_f, hg_i, hg_g, xa_q, xa_g = jnp.split(h, idx, axis=-1)
    heads = lambda a, n, d: a.reshape(b, t, n, d)
    f = lb + (1.0 - lb) * jax.nn.sigmoid(hg_f.astype(F32))
    hg_logf = heads(jnp.log(f), HG_HEADS, HG_DK)
    hg_k = heads(1.0 - f, HG_HEADS, HG_DK)
    return (heads(sb_q, SB_HEADS, SB_DH), heads(sb_k, SB_HEADS, SB_DH), heads(sb_v, SB_HEADS, SB_DH), sb_g,
            heads(hg_q.astype(F32), HG_HEADS, HG_DK), hg_logf, hg_k, heads(hg_i.astype(F32), HG_HEADS, HG_DV), hg_g,
            heads(xa_q, XA_HEADS, XA_DH), xa_g)


def _sb_attend(q, k, v, bias, q_start):
    tq, s = q.shape[1], k.shape[1]
    z = jnp.einsum('bqhd,bkhd->bhqk', q.astype(F32), k.astype(F32)) / math.sqrt(SB_DH)
    z = z + bias.astype(F32)[None, :, None, None]
    t_pos = q_start + jnp.arange(tq)
    causal = (jnp.arange(s)[None, :] < t_pos[:, None])[None, None]
    log_beta = jax.nn.log_sigmoid(z)
    log_1m = jnp.where(causal, jax.nn.log_sigmoid(-z), 0.0)
    suffix = lax.cumsum(log_1m, axis=3, reverse=True) - log_1m
    a = jnp.exp(jnp.where(causal, log_beta + suffix, -jnp.inf))
    return jnp.einsum('bhqk,bkhd->bqhd', a, v.astype(F32)).astype(q.dtype)


def _sb_prompt(q, k, v, bias):
    b, s, h, d = q.shape
    nb = s // Q_BLOCK
    qb = q.reshape(b, nb, Q_BLOCK, h, d).swapaxes(0, 1)
    out = lax.map(lambda a: _sb_attend(a[1], k, v, bias, a[0] * Q_BLOCK), (jnp.arange(nb), qb))
    return out.swapaxes(0, 1).reshape(b, s, h, d)


def _hgrn_chunk(s0, q, logf, k, v):
    c = q.shape[1]
    bc = jnp.cumsum(logf, axis=1)
    mask = jnp.tril(jnp.ones((c, c), bool))[None, :, :, None, None]
    diff = bc[:, :, None] - bc[:, None, :]
    decay = jnp.exp(jnp.where(mask, diff, -jnp.inf))
    attn = jnp.einsum('bthd,bshd,btshd->bhts', q, k, decay)
    o = jnp.einsum('bhts,bshv->bthv', attn, v) + jnp.einsum('bthd,bhdv->bthv', q * jnp.exp(bc), s0)
    last = bc[:, -1]
    k_dec = k * jnp.exp(last[:, None] - bc)
    s_new = jnp.exp(last)[..., None] * s0 + jnp.einsum('bshd,bshv->bhdv', k_dec, v)
    return o, s_new


def _hgrn_prompt(q, logf, k, v):
    b, s = q.shape[0], q.shape[1]
    nc = s // HG_CHUNK
    chunks = lambda a: a.reshape(b, nc, HG_CHUNK, a.shape[2], a.shape[3]).swapaxes(0, 1)
    s0 = jnp.zeros((b, HG_HEADS, HG_DK, HG_DV), F32)

    def step(state, xs):
        o, st = _hgrn_chunk(state, *xs)
        return st, o

    s_fin, o = lax.scan(step, s0, (chunks(q), chunks(logf), chunks(k), chunks(v)))
    return o.swapaxes(0, 1).reshape(b, s, HG_HEADS, HG_DV), s_fin


def _mem_kv(mem, g, w):
    b, m, _ = mem.shape
    mk, mv = jnp.split(_rmsnorm(mem, g) @ w, 2, axis=-1)
    return mk.reshape(b, m, XA_HEADS, XA_DH), mv.reshape(b, m, XA_HEADS, XA_DH)


def _cross_attend(q, mk, mv):
    s = jnp.einsum('bthd,bmhd->bhtm', q.astype(F32), mk.astype(F32)) / math.sqrt(XA_DH)
    p = jax.nn.softmax(s, axis=-1)
    return jnp.einsum('bhtm,bmhd->bthd', p, mv.astype(F32)).astype(q.dtype)


def _mixer_output(x, sb_o, sb_g, hg_o, hg_g, hg_norm_g, xa_o, xa_g, w_out):
    b, t = sb_o.shape[0], sb_o.shape[1]
    sb = sb_o.reshape(b, t, SB_W).astype(F32) * jax.nn.silu(sb_g.astype(F32))
    hg = _rmsnorm(hg_o, hg_norm_g.reshape(HG_HEADS, HG_DV)).reshape(b, t, HG_WV) * jax.nn.silu(hg_g.astype(F32))
    xa = xa_o.reshape(b, t, XA_W).astype(F32) * jax.nn.silu(xa_g.astype(F32))
    mix = jnp.concatenate([sb, hg, xa], axis=-1).astype(x.dtype)
    return x + mix @ w_out


def setup_inputs(seed: int = 0) -> dict:
    key = jax.random.key(seed)
    ks = jax.random.split(key, 20)
    n_pages = PAST_LEN // PAGE_SIZE
    n_pool = (DEC_BATCH * n_pages * 5) // 4
    nrm = lambda k, shp: jax.random.normal(k, shp, F32)
    page_table = jax.random.permutation(ks[5], n_pool)[:DEC_BATCH * n_pages].reshape(DEC_BATCH, n_pages).astype(jnp.int32)
    return {
        'x_prompt': nrm(ks[0], (BATCH, SEQ, D_MODEL)),
        'x_sample': nrm(ks[1], (DEC_BATCH, DEC_SEQ, D_MODEL)),
        'mem_prompt': nrm(ks[2], (BATCH, N_MEM, D_MODEL)),
        'cache_k': nrm(ks[3], (DEPTH, n_pool, PAGE_SIZE, SB_HEADS, SB_DH)),
        'cache_v': nrm(ks[4], (DEPTH, n_pool, PAGE_SIZE, SB_HEADS, SB_DH)),
        'page_table': page_table,
        'state_hgrn': 0.5 * nrm(ks[6], (DEPTH, DEC_BATCH, HG_HEADS, HG_DK, HG_DV)),
        'cache_mem_k': nrm(ks[7], (DEPTH, DEC_BATCH, N_MEM, XA_HEADS, XA_DH)),
        'cache_mem_v': nrm(ks[8], (DEPTH, DEC_BATCH, N_MEM, XA_HEADS, XA_DH)),
        'norm_gain': 1.0 + 0.02 * nrm(ks[9], (DEPTH, D_MODEL)),
        'w_in': nrm(ks[10], (DEPTH, D_MODEL, D_IN)) * D_MODEL ** -0.5,
        'sb_bias': SB_BIAS_INIT + 0.3 * nrm(ks[17], (DEPTH, SB_HEADS)),
        'hg_lb_logits': 0.5 * nrm(ks[11], (DEPTH + 1, HG_WK)),
        'hg_norm_gain': 1.0 + 0.02 * nrm(ks[12], (DEPTH, HG_WV)),
        'mem_norm_gain': 1.0 + 0.02 * nrm(ks[13], (DEPTH, D_MODEL)),
        'w_mem_kv': nrm(ks[14], (DEPTH, D_MODEL, 2 * XA_W)) * D_MODEL ** -0.5,
        'w_out': nrm(ks[15], (DEPTH, D_MIX, D_MODEL)) * D_MIX ** -0.5,
        'final_norm_gain': 1.0 + 0.02 * nrm(ks[16], (D_MODEL,)),
    }


def reference(x_prompt, x_sample, mem_prompt, cache_k, cache_v, page_table, state_hgrn, cache_mem_k, cache_mem_v,
              norm_gain, w_in, sb_bias, hg_lb_logits, hg_norm_gain, mem_norm_gain, w_mem_kv, w_out, final_norm_gain):
    lb_all = jnp.cumsum(jax.nn.softmax(hg_lb_logits.astype(F32), axis=0), axis=0)
    n_pages = page_table.shape[1]
    dec_b = x_sample.shape[0]
    hp, hs = x_prompt, x_sample
    kp_l, vp_l, hgp_l, mkp_l, mvp_l, ks_l, vs_l, hgs_l = [], [], [], [], [], [], [], []
    for l in range(DEPTH):
        lb = lb_all[l]
        sq, sk, sv, sg, hq, hlf, hk, hv, hg, xq, xg = _mixer_inputs(hp, norm_gain[l], w_in[l], lb)
        sb_o = _sb_prompt(sq, sk, sv, sb_bias[l])
        hg_o, hg_st_p = _hgrn_prompt(hq, hlf, hk, hv)
        mk, mv = _mem_kv(mem_prompt, mem_norm_gain[l], w_mem_kv[l])
        xa_o = _cross_attend(xq, mk, mv)
        hp = _mixer_output(hp, sb_o, sg, hg_o, hg, hg_norm_gain[l], xa_o, xg, w_out[l])
        kp_l.append(sk); vp_l.append(sv); hgp_l.append(hg_st_p); mkp_l.append(mk); mvp_l.append(mv)
        sq, sk, sv, sg, hq, hlf, hk, hv, hg, xq, xg = _mixer_inputs(hs, norm_gain[l], w_in[l], lb)
        past_k = cache_k[l][page_table].reshape(dec_b, n_pages * PAGE_SIZE, SB_HEADS, SB_DH)
        past_v = cache_v[l][page_table].reshape(dec_b, n_pages * PAGE_SIZE, SB_HEADS, SB_DH)
        k_all = jnp.concatenate([past_k.astype(F32), sk.astype(F32)], axis=1)
        v_all = jnp.concatenate([past_v.astype(F32), sv.astype(F32)], axis=1)
        sb_o = _sb_attend(sq, k_all, v_all, sb_bias[l], n_pages * PAGE_SIZE)
        hg_o, hg_st_s = _hgrn_chunk(state_hgrn[l].astype(F32), hq, hlf, hk, hv)
        xa_o = _cross_attend(xq, cache_mem_k[l], cache_mem_v[l])
        hs = _mixer_output(hs, sb_o, sg, hg_o, hg, hg_norm_gain[l], xa_o, xg, w_out[l])
        ks_l.append(sk); vs_l.append(sv); hgs_l.append(hg_st_s.astype(state_hgrn.dtype))
    y_prompt = _rmsnorm(hp, final_norm_gain)
    y_sample = _rmsnorm(hs, final_norm_gain)
    k_prompt = jnp.stack(kp_l); v_prompt = jnp.stack(vp_l); hgrn_prompt = jnp.stack(hgp_l)
    mem_k_prompt = jnp.stack(mkp_l); mem_v_prompt = jnp.stack(mvp_l)
    k_sample = jnp.stack(ks_l); v_sample = jnp.stack(vs_l); hgrn_sample = jnp.stack(hgs_l)
    return (y_prompt, y_sample, k_prompt, v_prompt, hgrn_prompt, mem_k_prompt, mem_v_prompt, k_sample, v_sample, hgrn_sample)
```

```python
import functools
import math

import numpy as np
import jax
import jax.numpy as jnp
from jax import lax
from jax.experimental import pallas as pl
from jax.experimental.pallas import tpu as pltpu

F32 = jnp.float32
BF16 = jnp.bfloat16
EPS = 1e-6

SB_HEADS, SB_DH = 8, 64
HG_HEADS, HG_D = 4, 64
XA_HEADS, XA_DH = 4, 64
SB_W = SB_HEADS * SB_DH
HG_W = HG_HEADS * HG_D
XA_W = XA_HEADS * XA_DH
D_REST = SB_W + 3 * HG_W + HG_W + 2 * XA_W

LANES = 128
SB_TILE = 128
HG_CHUNK = 128
PAGES_PER_STEP = 8
VMEM_LIMIT = 48 * 1024 * 1024

_NT = (((1,), (1,)), ((), ()))
_TN = (((0,), (0,)), ((), ()))


def _split_bf16(x):
    hi = x.astype(BF16)
    lo = (x - hi.astype(F32)).astype(BF16)
    return hi, lo


def _log_sigmoid_pair(t):
    sp = jnp.log1p(jnp.exp(-jnp.abs(t)))
    log_beta = jnp.minimum(t, 0.0) - sp
    return log_beta, log_beta - t


def _norm_proj_kernel(x_ref, g_ref, w_ref, *out_refs, outs):
    x = x_ref[...]
    ms = jnp.mean(x * x, axis=-1, keepdims=True)
    xn = ((x * lax.rsqrt(ms + EPS)) * g_ref[...]).astype(BF16)
    cache = {}
    for o_ref, (lo, hi, scale) in zip(out_refs, outs):
        if (lo, hi) not in cache:
            cache[(lo, hi)] = jnp.dot(xn, w_ref[:, lo:hi], preferred_element_type=F32)
        h = cache[(lo, hi)]
        if scale != 1.0:
            h = h * scale
        o_ref[...] = h.astype(o_ref.dtype)


def _norm_proj(x, gain, w_bf, outs, tm):
    m, d = x.shape
    n = w_bf.shape[1]
    tm = min(tm, m)
    kern = functools.partial(_norm_proj_kernel, outs=[(lo, hi, sc) for lo, hi, _, sc in outs])
    return pl.pallas_call(
        kern,
        grid=(m // tm,),
        in_specs=[pl.BlockSpec((tm, d), lambda i: (i, 0)),
                  pl.BlockSpec((1, d), lambda i: (0, 0)),
                  pl.BlockSpec((d, n), lambda i: (0, 0))],
        out_specs=[pl.BlockSpec((tm, hi - lo), lambda i: (i, 0)) for lo, hi, _, _ in outs],
        out_shape=[jax.ShapeDtypeStruct((m, hi - lo), dt) for lo, hi, dt, _ in outs],
        compiler_params=pltpu.CompilerParams(dimension_semantics=("arbitrary",), vmem_limit_bytes=VMEM_LIMIT),
    )(x, gain.reshape(1, d), w_bf)


def _suffix_matrix():
    jp = np.arange(LANES)[:, None]
    j = np.arange(LANES)[None, :]
    w = np.concatenate([(jp > j).astype(np.float32), np.ones((LANES, LANES), np.float32)], axis=1)
    return jnp.asarray(np.concatenate([w, w], axis=0), dtype=BF16)


def _sb_prompt_kernel(bias_ref, q_ref, k_ref, v_ref, w2_ref, o_ref, vbd_ref, carry_ref, acc_ref):
    t_ = SB_TILE
    hp = pl.program_id(1)
    qi = pl.program_id(2)
    nkb = k_ref.shape[0] // t_

    @pl.when(qi == 0)
    def _():
        lane_v = lax.broadcasted_iota(jnp.int32, (t_, LANES), 1)

        def body(j, c):
            vb = v_ref[pl.ds(pl.multiple_of(j * t_, t_), t_), :]
            vbd_ref[j, 0:t_, :] = jnp.where(lane_v < SB_DH, vb, jnp.zeros_like(vb))
            vbd_ref[j, t_:2 * t_, :] = jnp.where(lane_v >= SB_DH, vb, jnp.zeros_like(vb))
            return c

        lax.fori_loop(0, nkb, body, 0)

    q2 = q_ref[...]
    lane_q = lax.broadcasted_iota(jnp.int32, (t_, LANES), 1)
    zq = jnp.zeros_like(q2)
    q_st = jnp.concatenate([jnp.where(lane_q < SB_DH, q2, zq), jnp.where(lane_q >= SB_DH, q2, zq)], axis=0)
    row = lax.broadcasted_iota(jnp.int32, (2 * t_, LANES), 0)
    col = lax.broadcasted_iota(jnp.int32, (2 * t_, LANES), 1)
    bias_t = jnp.where(row < t_, bias_ref[2 * hp], bias_ref[2 * hp + 1]).astype(F32)
    causal = col < (row & (t_ - 1))

    carry_ref[...] = jnp.zeros_like(carry_ref)
    acc_ref[...] = jnp.zeros_like(acc_ref)

    def block(j, masked):
        kb = k_ref[pl.ds(pl.multiple_of(j * t_, t_), t_), :]
        z = lax.dot_general(q_st, kb, _NT, preferred_element_type=F32)
        log_beta, log_1m = _log_sigmoid_pair(z + bias_t)
        if masked:
            log_1m = jnp.where(causal, log_1m, 0.0)
        hi, lo = _split_bf16(log_1m)
        r = jnp.dot(jnp.concatenate([hi, lo], axis=1), w2_ref[...], preferred_element_type=F32)
        suffix = r[:, :LANES] + carry_ref[...]
        carry_ref[...] += r[:, LANES:]
        a = jnp.exp(log_beta + suffix)
        if masked:
            a = jnp.where(causal, a, 0.0)
        a = a.astype(BF16)
        a2 = jnp.concatenate([a[:t_], a[t_:]], axis=1)
        acc_ref[...] += jnp.dot(a2, vbd_ref[j], preferred_element_type=F32)

    block(qi, True)

    def loop_body(n, c):
        block(qi - 1 - n, False)
        return c

    lax.fori_loop(0, qi, loop_body, 0)
    o_ref[...] = acc_ref[...]


def _sb_prompt(q_bf, k_bf, v_bf, bias, batch, seq):
    t_ = SB_TILE
    nq = seq // t_
    npairs = SB_W // LANES
    return pl.pallas_call(
        _sb_prompt_kernel,
        grid=(batch, npairs, nq),
        in_specs=[pl.BlockSpec(memory_space=pltpu.SMEM),
                  pl.BlockSpec((t_, LANES), lambda b, h, i: (b * nq + i, h)),
                  pl.BlockSpec((seq, LANES), lambda b, h, i: (b, h)),
                  pl.BlockSpec((seq, LANES), lambda b, h, i: (b, h)),
                  pl.BlockSpec((2 * LANES, 2 * LANES), lambda b, h, i: (0, 0))],
        out_specs=pl.BlockSpec((t_, LANES), lambda b, h, i: (b * nq + i, h)),
        out_shape=jax.ShapeDtypeStruct((batch * seq, SB_W), F32),
        scratch_shapes=[pltpu.VMEM((nq, 2 * t_, LANES), BF16),
                        pltpu.VMEM((2 * t_, LANES), F32),
                        pltpu.VMEM((t_, LANES), F32)],
        compiler_params=pltpu.CompilerParams(dimension_semantics=("arbitrary", "arbitrary", "arbitrary"),
                                             vmem_limit_bytes=VMEM_LIMIT),
    )(bias, q_bf, k_bf, v_bf, _suffix_matrix())


def _hgrn_levels(c):
    return [c >> (i + 1) for i in range(int(math.log2(c)))]


def _hgrn_constants(c):
    t = np.arange(c)[:, None]
    u = np.arange(c)[None, :]
    rows = [(u <= t), (u > t)]
    masks = []
    for hs in _hgrn_levels(c):
        mid = (t // (2 * hs)) * (2 * hs) + hs - 1
        upper = (t % (2 * hs)) >= hs
        rows.append(upper & (u > mid) & (u <= t))
        rows.append((~upper) & (u > t) & (u <= mid))
        s = u
        masks.append(((t // (2 * hs)) == (s // (2 * hs))) & upper & ((s % (2 * hs)) < hs))
    masks.append(t == u)
    mall = np.concatenate(rows, axis=0).astype(np.float32)
    mall2 = np.concatenate([mall, mall], axis=1)
    return jnp.asarray(mall2, dtype=BF16), jnp.asarray(np.stack(masks).astype(np.float32))


def _forget_lower_bound(lbl_ref):
    l0 = lbl_ref[0:1]
    l1 = lbl_ref[1:2]
    m = jnp.maximum(l0, l1)
    e0 = jnp.exp(l0 - m)
    e1 = jnp.exp(l1 - m)
    return e0 / (e0 + e1)


def _hgrn_prompt_kernel(lbl_ref, hq_ref, hf_ref, hv_ref, mall_ref, bm_ref, o_ref, st_ref, s_scr, d_scr):
    c = HG_CHUNK
    ci = pl.program_id(1)
    levels = _hgrn_levels(c)

    @pl.when(ci == 0)
    def _():
        s_scr[...] = jnp.zeros_like(s_scr)

    lb = _forget_lower_bound(lbl_ref)
    q = hq_ref[0]
    f = lb + (1.0 - lb) * jax.nn.sigmoid(hf_ref[0])
    logf = jnp.log(f)
    k = 1.0 - f
    v = hv_ref[0]
    hi, lo = _split_bf16(logf)
    d_scr[...] = jnp.dot(mall_ref[...], jnp.concatenate([hi, lo], axis=0), preferred_element_type=F32)

    lane = lax.broadcasted_iota(jnp.int32, (c, HG_W), 1)
    head_masks = [(lane >= h * HG_D) & (lane < (h + 1) * HG_D) for h in range(HG_HEADS)]

    def by_head(x_bf):
        z = jnp.zeros_like(x_bf)
        return jnp.concatenate([jnp.where(m, x_bf, z) for m in head_masks], axis=0)

    def pair_scores(q_f32, k_f32, mask):
        p = lax.dot_general(by_head(q_f32.astype(BF16)), k_f32.astype(BF16), _NT, preferred_element_type=F32)
        return p * jnp.concatenate([mask] * HG_HEADS, axis=0)

    attn = pair_scores(q, k, bm_ref[len(levels)])
    for li in range(len(levels)):
        dq = d_scr[(2 + 2 * li) * c:(3 + 2 * li) * c, :]
        dk = d_scr[(3 + 2 * li) * c:(4 + 2 * li) * c, :]
        attn = attn + pair_scores(q * jnp.exp(dq), k * jnp.exp(dk), bm_ref[li])
    a_bf = attn.astype(BF16)
    a2 = jnp.concatenate([a_bf[h * c:(h + 1) * c] for h in range(HG_HEADS)], axis=1)
    v_bf = v.astype(BF16)
    o = jnp.dot(a2, by_head(v_bf), preferred_element_type=F32)

    st = s_scr[...]
    bcum = d_scr[0:c, :]
    o = o + lax.dot_general((q * jnp.exp(bcum)).astype(BF16), st.astype(BF16), _NT, preferred_element_type=F32)
    o_ref[0] = o

    k_dec = (k * jnp.exp(d_scr[c:2 * c, :])).astype(BF16)
    upd = lax.dot_general(v_bf, k_dec, _TN, preferred_element_type=F32)
    r = lax.broadcasted_iota(jnp.int32, (HG_W, HG_W), 0)
    cc = lax.broadcasted_iota(jnp.int32, (HG_W, HG_W), 1)
    same_head = (r // HG_D) == (cc // HG_D)
    s_new = st * jnp.exp(bcum[c - 1:c, :]) + jnp.where(same_head, upd, 0.0)
    s_scr[...] = s_new

    @pl.when(ci == pl.num_programs(1) - 1)
    def _():
        st_ref[0] = s_new


def _hgrn_prompt(rest3, lb_logits):
    b, t, _ = rest3.shape
    c = HG_CHUNK
    mall2, bm = _hgrn_constants(c)
    nr = mall2.shape[0]
    col = lambda j: (lambda bi, ci: (bi, ci, j))
    return pl.pallas_call(
        _hgrn_prompt_kernel,
        grid=(b, t // c),
        in_specs=[pl.BlockSpec((2, HG_W), lambda bi, ci: (0, 0)),
                  pl.BlockSpec((1, c, HG_W), col(2)),
                  pl.BlockSpec((1, c, HG_W), col(3)),
                  pl.BlockSpec((1, c, HG_W), col(4)),
                  pl.BlockSpec((nr, 2 * c), lambda bi, ci: (0, 0)),
                  pl.BlockSpec(bm.shape, lambda bi, ci: (0, 0, 0))],
        out_specs=[pl.BlockSpec((1, c, HG_W), lambda bi, ci: (bi, ci, 0)),
                   pl.BlockSpec((1, HG_W, HG_W), lambda bi, ci: (bi, 0, 0))],
        out_shape=[jax.ShapeDtypeStruct((b, t, HG_W), F32), jax.ShapeDtypeStruct((b, HG_W, HG_W), F32)],
        scratch_shapes=[pltpu.VMEM((HG_W, HG_W), F32), pltpu.VMEM((nr, HG_W), F32)],
        compiler_params=pltpu.CompilerParams(dimension_semantics=("arbitrary", "arbitrary"),
                                             vmem_limit_bytes=VMEM_LIMIT),
    )(lb_logits, rest3, rest3, rest3, mall2, bm)


def _silu(g):
    return g * jax.nn.sigmoid(g)


def _mix_out_kernel(x_ref, sbo_ref, hgo_ref, sbg_ref, hgg_ref, xaq_ref, xag_ref, mk_ref, mv_ref,
                    wout_ref, hgn_ref, bd2_ref, fng_ref, y_ref):
    tm = x_ref.shape[1]
    rows = max(tm, 8)

    def load(ref):
        a = ref[0]
        return jnp.broadcast_to(a, (rows, a.shape[-1])) if tm < rows else a

    x = load(x_ref)
    lane = lax.broadcasted_iota(jnp.int32, (rows, XA_W), 1)
    lane_m = lax.broadcasted_iota(jnp.int32, (mk_ref.shape[1], XA_W), 1)

    xq = (load(xaq_ref) * (1.0 / math.sqrt(XA_DH))).astype(BF16)
    zq = jnp.zeros_like(xq)
    q_st = jnp.concatenate([jnp.where((lane >= h * XA_DH) & (lane < (h + 1) * XA_DH), xq, zq)
                            for h in range(XA_HEADS)], axis=0)
    s = lax.dot_general(q_st, mk_ref[0].astype(BF16), _NT, preferred_element_type=F32)
    p = jnp.exp(s - jnp.max(s, axis=-1, keepdims=True))
    p = (p / jnp.sum(p, axis=-1, keepdims=True)).astype(BF16)
    p2 = jnp.concatenate([p[h * rows:(h + 1) * rows] for h in range(XA_HEADS)], axis=1)
    mv = mv_ref[0].astype(BF16)
    zv = jnp.zeros_like(mv)
    mv_st = jnp.concatenate([jnp.where((lane_m >= h * XA_DH) & (lane_m < (h + 1) * XA_DH), mv, zv)
                             for h in range(XA_HEADS)], axis=0)
    xa_o = jnp.dot(p2, mv_st, preferred_element_type=F32)

    hg = load(hgo_ref)
    hi, lo = _split_bf16(hg * hg)
    ms = jnp.dot(jnp.concatenate([hi, lo], axis=1), bd2_ref[...], preferred_element_type=F32) * (1.0 / HG_D)
    hg_n = (hg * lax.rsqrt(ms + EPS)) * hgn_ref[...]

    sb = (load(sbo_ref) * _silu(load(sbg_ref))).astype(BF16)
    hgm = (hg_n * _silu(load(hgg_ref))).astype(BF16)
    xam = (xa_o * _silu(load(xag_ref))).astype(BF16)
    y = x + jnp.dot(sb, wout_ref[0:SB_W, :], preferred_element_type=F32)
    y = y + jnp.dot(hgm, wout_ref[SB_W:SB_W + HG_W, :], preferred_element_type=F32)
    y = y + jnp.dot(xam, wout_ref[SB_W + HG_W:, :], preferred_element_type=F32)
    msy = jnp.mean(y * y, axis=-1, keepdims=True)
    y = (y * lax.rsqrt(msy + EPS)) * fng_ref[...]
    y_ref[0] = y[0:tm]


def _mix_out(x3, sbo3, hgo3, rest3, mk3, mv3, wout_bf, hg_norm_gain, final_gain, tm):
    b, t, d = x3.shape
    nm = mk3.shape[1]
    bd = (np.arange(HG_W)[:, None] // HG_D) == (np.arange(HG_W)[None, :] // HG_D)
    bd2 = jnp.asarray(np.concatenate([bd, bd], axis=0).astype(np.float32), dtype=BF16)
    blk = lambda w, j: pl.BlockSpec((1, tm, w), lambda bi, ti: (bi, ti, j))
    const = lambda shape: pl.BlockSpec(shape, lambda bi, ti: (0,) * len(shape))
    return pl.pallas_call(
        _mix_out_kernel,
        grid=(b, t // tm),
        in_specs=[blk(d, 0), blk(SB_W, 0), blk(HG_W, 0),
                  blk(SB_W, 0),
                  blk(HG_W, 5),
                  blk(XA_W, 6),
                  blk(XA_W, 7),
                  pl.BlockSpec((1, nm, XA_W), lambda bi, ti: (bi, 0, 0)),
                  pl.BlockSpec((1, nm, XA_W), lambda bi, ti: (bi, 0, 0)),
                  const(wout_bf.shape), const((1, HG_W)), const(bd2.shape), const((1, d))],
        out_specs=blk(d, 0),
        out_shape=jax.ShapeDtypeStruct((b, t, d), F32),
        compiler_params=pltpu.CompilerParams(dimension_semantics=("arbitrary", "arbitrary"),
                                             vmem_limit_bytes=VMEM_LIMIT),
    )(x3, sbo3, hgo3, rest3, rest3, rest3, rest3, mk3, mv3, wout_bf,
      hg_norm_gain.reshape(1, HG_W), bd2, final_gain.reshape(1, d))


def _page_matrices():
    n = PAGES_PER_STEP * SB_HEADS
    r = np.arange(n)[:, None]
    c = np.arange(n)[None, :]
    same_head = (r % SB_HEADS) == (c % SB_HEADS)
    before = same_head & ((c // SB_HEADS) < (r // SB_HEADS))
    mats = [np.concatenate([m, m], axis=1).astype(np.float32) for m in (before, same_head)]
    return jnp.asarray(np.concatenate(mats, axis=0), dtype=BF16)


def _sb_sample_kernel(pt_ref, q_ref, bias_ref, w2_ref, pm_ref, *refs):
    npg = PAGES_PER_STEP
    k_refs, v_refs = refs[:npg], refs[npg:2 * npg]
    o_ref, carry_ref, acc_ref = refs[2 * npg:]
    s = pl.program_id(1)
    n = npg * SB_HEADS

    @pl.when(s == 0)
    def _():
        carry_ref[...] = jnp.zeros_like(carry_ref)
        acc_ref[...] = jnp.zeros_like(acc_ref)

    row8 = lax.broadcasted_iota(jnp.int32, (SB_HEADS, SB_W), 0)
    lane8 = lax.broadcasted_iota(jnp.int32, (SB_HEADS, SB_W), 1)
    own = (lane8 // SB_DH) == row8
    qb = jnp.broadcast_to(q_ref[0].astype(F32), (SB_HEADS, SB_W))
    q_bd = jnp.where(own, qb, 0.0).astype(BF16)

    z = jnp.concatenate([lax.dot_general(q_bd, k_refs[i][0].astype(BF16), _NT, preferred_element_type=F32)
                         for i in range(npg)], axis=0)
    bias_t = jnp.concatenate([bias_ref[...]] * npg, axis=0)
    log_beta, log_1m = _log_sigmoid_pair(z + bias_t)
    hi, lo = _split_bf16(log_1m)
    r = jnp.dot(jnp.concatenate([hi, lo], axis=1), w2_ref[...], preferred_element_type=F32)
    tot = r[:, LANES:]
    thi, tlo = _split_bf16(tot)
    cross = jnp.dot(pm_ref[...], jnp.concatenate([thi, tlo], axis=0), preferred_element_type=F32)
    carry = carry_ref[...]
    suffix = r[:, :LANES] + cross[:n] + jnp.concatenate([carry] * npg, axis=0)
    carry_ref[...] = carry + cross[n:n + SB_HEADS]
    a = jnp.exp(log_beta + suffix).astype(BF16)
    acc = acc_ref[...]
    for i in range(npg):
        acc = acc + jnp.dot(a[i * SB_HEADS:(i + 1) * SB_HEADS], v_refs[i][0].astype(BF16),
                            preferred_element_type=F32)
    acc_ref[...] = acc

    @pl.when(s == pl.num_programs(1) - 1)
    def _():
        o_ref[0] = jnp.sum(jnp.where(own, acc, 0.0), axis=0, keepdims=True)


def _sb_sample(q_bf, bias, page_table, ck, cv):
    b = q_bf.shape[0]
    n_pages = page_table.shape[1]
    page = ck.shape[1]
    assert page == LANES and n_pages % PAGES_PER_STEP == 0
    npg = PAGES_PER_STEP
    nsteps = n_pages // npg

    def page_spec(i):
        return pl.BlockSpec((1, page, SB_W),
                            lambda bi, si, pt: (pt[bi * n_pages + (n_pages - 1 - (si * npg + i))], 0, 0))

    n = npg * SB_HEADS
    grid_spec = pltpu.PrefetchScalarGridSpec(
        num_scalar_prefetch=1,
        grid=(b, nsteps),
        in_specs=[pl.BlockSpec((1, 1, SB_W), lambda bi, si, pt: (bi, 0, 0)),
                  pl.BlockSpec((SB_HEADS, LANES), lambda bi, si, pt: (0, 0)),
                  pl.BlockSpec((2 * LANES, 2 * LANES), lambda bi, si, pt: (0, 0)),
                  pl.BlockSpec((2 * n, 2 * n), lambda bi, si, pt: (0, 0))]
                 + [page_spec(i) for i in range(npg)] * 2,
        out_specs=pl.BlockSpec((1, 1, SB_W), lambda bi, si, pt: (bi, 0, 0)),
        scratch_shapes=[pltpu.VMEM((SB_HEADS, LANES), F32), pltpu.VMEM((SB_HEADS, SB_W), F32)])
    bias_rep = jnp.broadcast_to(bias.astype(F32)[:, None], (SB_HEADS, LANES))
    out = pl.pallas_call(
        _sb_sample_kernel,
        grid_spec=grid_spec,
        out_shape=jax.ShapeDtypeStruct((b, 1, SB_W), F32),
        compiler_params=pltpu.CompilerParams(dimension_semantics=("arbitrary", "arbitrary"),
                                             vmem_limit_bytes=VMEM_LIMIT),
    )(page_table.reshape(-1), q_bf.reshape(b, 1, SB_W), bias_rep, _suffix_matrix(), _page_matrices(),
      *([ck] * npg), *([cv] * npg))
    return out.reshape(b, SB_W)


def _hgrn_sample_kernel(lbl_ref, s_ref, q_ref, f_ref, v_ref, o_ref, sn_ref):
    l0 = lbl_ref[0]
    l1 = lbl_ref[1]
    m = jnp.maximum(l0, l1)
    e0 = jnp.exp(l0 - m)
    e1 = jnp.exp(l1 - m)
    lb = e0 / (e0 + e1)
    f = lb + (1.0 - lb) * jax.nn.sigmoid(f_ref[0])
    k = 1.0 - f
    v = v_ref[0]
    v_rows = jnp.concatenate([jnp.broadcast_to(v[:, h * HG_D:(h + 1) * HG_D], (HG_D, HG_D))
                              for h in range(HG_HEADS)], axis=0)
    s_new = f * s_ref[0] + k * v_rows
    sn_ref[0] = s_new
    qs = q_ref[0] * s_new
    o_ref[0] = jnp.sum(qs.reshape(HG_HEADS, HG_D, HG_D), axis=1)


def _hgrn_sample(state, q_col, f_col, v_row, lb_logits):
    b = state.shape[0]
    row = lambda shape: pl.BlockSpec((1,) + shape, lambda bi: (bi, 0, 0))
    return pl.pallas_call(
        _hgrn_sample_kernel,
        grid=(b,),
        in_specs=[pl.BlockSpec((2, HG_W, 1), lambda bi: (0, 0, 0)),
                  row((HG_W, HG_D)), row((HG_W, 1)), row((HG_W, 1)), row((1, HG_W))],
        out_specs=[row((HG_HEADS, HG_D)), row((HG_W, HG_D))],
        out_shape=[jax.ShapeDtypeStruct((b, HG_HEADS, HG_D), F32), jax.ShapeDtypeStruct((b, HG_W, HG_D), F32)],
        compiler_params=pltpu.CompilerParams(dimension_semantics=("arbitrary",)),
    )(lb_logits.reshape(2, HG_W, 1), state, q_col, f_col, v_row)


def _in_proj_outs():
    qs = 1.0 / math.sqrt(SB_DH)
    return [(0, SB_W, BF16, qs),
            (SB_W, 2 * SB_W, BF16, 1.0),
            (2 * SB_W, 3 * SB_W, BF16, 1.0),
            (SB_W, 2 * SB_W, F32, 1.0),
            (2 * SB_W, 3 * SB_W, F32, 1.0),
            (3 * SB_W, 3 * SB_W + D_REST, F32, 1.0)]


def kernel(x_prompt, x_sample, mem_prompt, cache_k, cache_v, page_table, state_hgrn, cache_mem_k, cache_mem_v,
           norm_gain, w_in, sb_bias, hg_lb_logits, hg_norm_gain, mem_norm_gain, w_mem_kv, w_out, final_norm_gain):
    batch, seq, d = x_prompt.shape
    dec_b = x_sample.shape[0]
    n_mem = mem_prompt.shape[1]
    depth = w_in.shape[0]
    assert depth == 1 and x_sample.shape[1] == 1
    l = 0
    lb_logits = hg_lb_logits.astype(F32)
    assert lb_logits.shape[0] == 2
    w_in_bf = w_in[l].astype(BF16)
    w_out_bf = w_out[l].astype(BF16)
    w_mem_bf = w_mem_kv[l].astype(BF16)
    bias = sb_bias[l].astype(F32)

    q_bf, k_bf, v_bf, k_p, v_p, rest = _norm_proj(x_prompt.reshape(batch * seq, d), norm_gain[l], w_in_bf,
                                                  _in_proj_outs(), tm=256)
    mk, mv = _norm_proj(mem_prompt.reshape(batch * n_mem, d), mem_norm_gain[l], w_mem_bf,
                        [(0, XA_W, F32, 1.0), (XA_W, 2 * XA_W, F32, 1.0)], tm=256)
    sb_o = _sb_prompt(q_bf, k_bf, v_bf, bias, batch, seq)
    rest3 = rest.reshape(batch, seq, D_REST)
    hg_o, st_t = _hgrn_prompt(rest3, lb_logits)
    y_prompt = _mix_out(x_prompt, sb_o.reshape(batch, seq, SB_W), hg_o, rest3,
                        mk.reshape(batch, n_mem, XA_W), mv.reshape(batch, n_mem, XA_W),
                        w_out_bf, hg_norm_gain[l], final_norm_gain, tm=256)
    hgrn_prompt = jnp.stack([st_t[:, h * HG_D:(h + 1) * HG_D, h * HG_D:(h + 1) * HG_D]
                             for h in range(HG_HEADS)], axis=1).swapaxes(-1, -2)

    qs_bf, _, _, k_s, v_s, rest_s = _norm_proj(x_sample.reshape(dec_b, d), norm_gain[l], w_in_bf,
                                               _in_proj_outs(), tm=dec_b)
    n_pool, page = cache_k.shape[1], cache_k.shape[2]
    sb_o_s = _sb_sample(qs_bf, bias, page_table, cache_k[l].reshape(n_pool, page, SB_W),
                        cache_v[l].reshape(n_pool, page, SB_W))
    hq_s = rest_s[:, SB_W:SB_W + HG_W].reshape(dec_b, HG_W, 1)
    hf_s = rest_s[:, SB_W + HG_W:SB_W + 2 * HG_W].reshape(dec_b, HG_W, 1)
    hv_s = rest_s[:, SB_W + 2 * HG_W:SB_W + 3 * HG_W].reshape(dec_b, 1, HG_W)
    hg_o_s, st_s = _hgrn_sample(state_hgrn[l].astype(F32).reshape(dec_b, HG_W, HG_D), hq_s, hf_s, hv_s, lb_logits)
    y_sample = _mix_out(x_sample, sb_o_s.reshape(dec_b, 1, SB_W), hg_o_s.reshape(dec_b, 1, HG_W),
                        rest_s.reshape(dec_b, 1, D_REST),
                        cache_mem_k[l].reshape(dec_b, n_mem, XA_W), cache_mem_v[l].reshape(dec_b, n_mem, XA_W),
                        w_out_bf, hg_norm_gain[l], final_norm_gain, tm=1)

    shp = (1, batch, seq, SB_HEADS, SB_DH)
    return (y_prompt, y_sample, k_p.reshape(shp), v_p.reshape(shp),
            hgrn_prompt.reshape(1, batch, HG_HEADS, HG_D, HG_D),
            mk.reshape(1, batch, n_mem, XA_HEADS, XA_DH), mv.reshape(1, batch, n_mem, XA_HEADS, XA_DH),
            k_s.reshape(1, dec_b, 1, SB_HEADS, SB_DH), v_s.reshape(1, dec_b, 1, SB_HEADS, SB_DH),
            st_s.reshape(1, dec_b, HG_HEADS, HG_D, HG_D).astype(state_hgrn.dtype))
```

```python
import functools
import math

import numpy as np
import jax
import jax.numpy as jnp
from jax import lax
from jax.experimental import pallas as pl
from jax.experimental.pallas import tpu as pltpu

F32 = jnp.float32
BF16 = jnp.bfloat16
EPS = 1e-6

SB_HEADS, SB_DH = 8, 64
HG_HEADS, HG_D = 4, 64
XA_HEADS, XA_DH = 4, 64
SB_W = SB_HEADS * SB_DH
HG_W = HG_HEADS * HG_D
XA_W = XA_HEADS * XA_DH
D_REST = SB_W + 3 * HG_W + HG_W + 2 * XA_W

LANES = 128
SB_TQ = 256
SB_TK = 128
NEG_BIG = -1e30
LOG2E = math.log2(math.e)
HG_CHUNK = 128
PAGES_PER_STEP = 8
VMEM_LIMIT = 48 * 1024 * 1024

_NT = (((1,), (1,)), ((), ()))
_TN = (((0,), (0,)), ((), ()))


def _split_bf16(x):
    hi = x.astype(BF16)
    lo = (x - hi.astype(F32)).astype(BF16)
    return hi, lo


def _log_sigmoid_pair(t):
    sp = jnp.log1p(jnp.exp(-jnp.abs(t)))
    log_beta = jnp.minimum(t, 0.0) - sp
    return log_beta, log_beta - t


def _norm_proj_kernel(x_ref, g_ref, w_ref, *out_refs, outs):
    x = x_ref[...]
    ms = jnp.mean(x * x, axis=-1, keepdims=True)
    xn = ((x * lax.rsqrt(ms + EPS)) * g_ref[...]).astype(BF16)
    cache = {}
    for o_ref, (lo, hi, scale) in zip(out_refs, outs):
        if (lo, hi) not in cache:
            cache[(lo, hi)] = jnp.dot(xn, w_ref[:, lo:hi], preferred_element_type=F32)
        h = cache[(lo, hi)]
        if scale != 1.0:
            h = h * scale
        o_ref[...] = h.astype(o_ref.dtype)


def _norm_proj(x, gain, w_bf, outs, tm):
    m, d = x.shape
    n = w_bf.shape[1]
    tm = min(tm, m)
    kern = functools.partial(_norm_proj_kernel, outs=[(lo, hi, sc) for lo, hi, _, sc in outs])
    return pl.pallas_call(
        kern,
        grid=(m // tm,),
        in_specs=[pl.BlockSpec((tm, d), lambda i: (i, 0)),
                  pl.BlockSpec((1, d), lambda i: (0, 0)),
                  pl.BlockSpec((d, n), lambda i: (0, 0))],
        out_specs=[pl.BlockSpec((tm, hi - lo), lambda i: (i, 0)) for lo, hi, _, _ in outs],
        out_shape=[jax.ShapeDtypeStruct((m, hi - lo), dt) for lo, hi, dt, _ in outs],
        compiler_params=pltpu.CompilerParams(dimension_semantics=("arbitrary",), vmem_limit_bytes=VMEM_LIMIT),
    )(x, gain.reshape(1, d), w_bf)


def _suffix_matrix():
    jp = np.arange(LANES)[:, None]
    j = np.arange(LANES)[None, :]
    w = np.concatenate([(jp > j).astype(np.float32), np.ones((LANES, LANES), np.float32)], axis=1)
    return jnp.asarray(np.concatenate([w, w], axis=0), dtype=BF16)


def _sb_prompt_kernel(bias_ref, q_ref, k_ref, v_ref, w2_ref, o_ref, kbd_ref, vbd_ref, carry_ref, acc_ref, lb_scr, x_scr):
    tq, tk = SB_TQ, SB_TK
    hp = pl.program_id(1)
    qi = pl.program_id(2)
    nkb = k_ref.shape[0] // tk

    @pl.when(qi == 0)
    def _():
        lane_v = lax.broadcasted_iota(jnp.int32, (tk, LANES), 1)

        def body(j, c):
            vb = v_ref[pl.ds(pl.multiple_of(j * tk, tk), tk), :]
            vbd_ref[j, 0:tk, :] = jnp.where(lane_v < SB_DH, vb, jnp.zeros_like(vb))
            vbd_ref[j, tk:2 * tk, :] = jnp.where(lane_v >= SB_DH, vb, jnp.zeros_like(vb))
            kb = k_ref[pl.ds(pl.multiple_of(j * tk, tk), tk), :]
            kbd_ref[j, 0:tk, :] = jnp.where(lane_v < SB_DH, kb, jnp.zeros_like(kb))
            kbd_ref[j, tk:2 * tk, :] = jnp.where(lane_v >= SB_DH, kb, jnp.zeros_like(kb))
            return c

        lax.fori_loop(0, nkb, body, 0)

    q2 = q_ref[...]
    b2 = (bias_ref[2 * hp] * LOG2E, bias_ref[2 * hp + 1] * LOG2E)
    qpos = qi * tq + lax.broadcasted_iota(jnp.int32, (tq, LANES), 0)
    col = lax.broadcasted_iota(jnp.int32, (tq, LANES), 1)

    carry_ref[...] = jnp.zeros_like(carry_ref)
    acc_ref[...] = jnp.zeros_like(acc_ref)
    grp = tq // tk

    def block_of(m, g):
        return (qi - m) * grp + (grp - 1 - g)

    def finish_scores(z, j, slot, g, masked):
        causal = (j * tk + col) < qpos if masked else None
        for half in range(2):
            rows = slice(half * tq, (half + 1) * tq)
            t = z[:, half * tk:(half + 1) * tk] * LOG2E + b2[half]
            sp = jnp.log(1.0 + jnp.exp2(-jnp.abs(t))) * LOG2E
            lb = jnp.minimum(t, 0.0) - sp
            l1m = lb - t
            if masked:
                l1m = jnp.where(causal, l1m, 0.0)
                lb = jnp.where(causal, lb, NEG_BIG)
            hi, lo = _split_bf16(l1m)
            lb_scr[slot, g, rows, :] = lb
            x_scr[slot, g, rows, 0:LANES] = hi
            x_scr[slot, g, rows, LANES:2 * LANES] = lo

    def finish_weights(r, j, slot, g):
        suffix = r[:, :LANES] + carry_ref[...]
        carry_ref[...] += r[:, LANES:]
        a = jnp.exp2(lb_scr[slot, g] + suffix).astype(BF16)
        a2 = jnp.concatenate([a[:tq], a[tq:]], axis=1)
        acc_ref[...] += jnp.dot(a2, vbd_ref[j], preferred_element_type=F32)

    def stage(m_w, slot_w, m_s, slot_s, masked=False):
        rs, zs = [], []
        for g in range(grp):
            if m_w is not None:
                rs.append(jnp.dot(x_scr[slot_w, g], w2_ref[...], preferred_element_type=F32))
            if m_s is not None:
                zs.append(lax.dot_general(q2, kbd_ref[block_of(m_s, g)], _NT, preferred_element_type=F32))
        for g in range(grp):
            if m_w is not None:
                finish_weights(rs[g], block_of(m_w, g), slot_w, g)
        for g in range(grp):
            if m_s is not None:
                finish_scores(zs[g], block_of(m_s, g), slot_s, g, masked)

    stage(None, None, 0, 0, masked=True)

    def loop_body(p, c):
        stage(2 * p, 0, 2 * p + 1, 1)
        stage(2 * p + 1, 1, 2 * p + 2, 0)
        return c

    lax.fori_loop(0, qi // 2, loop_body, 0)

    @pl.when(qi % 2 == 1)
    def _():
        stage(qi - 1, 0, qi, 1)
        stage(qi, 1, None, None)

    @pl.when(qi % 2 == 0)
    def _():
        stage(qi, 0, None, None)

    o_ref[...] = acc_ref[...]


def _sb_prompt(q_bf, k_bf, v_bf, bias, batch, seq):
    tq, tk = SB_TQ, SB_TK
    assert seq % tq == 0 and tq % tk == 0
    nq = seq // tq
    npairs = SB_W // LANES
    return pl.pallas_call(
        _sb_prompt_kernel,
        grid=(batch, npairs, nq),
        in_specs=[pl.BlockSpec(memory_space=pltpu.SMEM),
                  pl.BlockSpec((tq, LANES), lambda b, h, i: (b * nq + i, h)),
                  pl.BlockSpec((seq, LANES), lambda b, h, i: (b, h)),
                  pl.BlockSpec((seq, LANES), lambda b, h, i: (b, h)),
                  pl.BlockSpec((2 * LANES, 2 * LANES), lambda b, h, i: (0, 0))],
        out_specs=pl.BlockSpec((tq, LANES), lambda b, h, i: (b * nq + i, h)),
        out_shape=jax.ShapeDtypeStruct((batch * seq, SB_W), F32),
        scratch_shapes=[pltpu.VMEM((seq // tk, 2 * tk, LANES), BF16),
                        pltpu.VMEM((seq // tk, 2 * tk, LANES), BF16),
                        pltpu.VMEM((2 * tq, LANES), F32),
                        pltpu.VMEM((tq, LANES), F32),
                        pltpu.VMEM((2, tq // tk, 2 * tq, LANES), F32),
                        pltpu.VMEM((2, tq // tk, 2 * tq, 2 * LANES), BF16)],
        compiler_params=pltpu.CompilerParams(dimension_semantics=("arbitrary", "arbitrary", "arbitrary"),
                                             vmem_limit_bytes=VMEM_LIMIT),
    )(bias, q_bf, k_bf, v_bf, _suffix_matrix())


def _hgrn_levels(c):
    return [c >> (i + 1) for i in range(int(math.log2(c)))]


def _hgrn_constants(c):
    t = np.arange(c)[:, None]
    u = np.arange(c)[None, :]
    rows = [(u <= t), (u > t)]
    masks = []
    for hs in _hgrn_levels(c):
        mid = (t // (2 * hs)) * (2 * hs) + hs - 1
        upper = (t % (2 * hs)) >= hs
        rows.append(upper & (u > mid) & (u <= t))
        rows.append((~upper) & (u > t) & (u <= mid))
        s = u
        masks.append(((t // (2 * hs)) == (s // (2 * hs))) & upper & ((s % (2 * hs)) < hs))
    masks.append(t == u)
    mall = np.concatenate(rows, axis=0).astype(np.float32)
    mall2 = np.concatenate([mall, mall], axis=1)
    return jnp.asarray(mall2, dtype=BF16), jnp.asarray(np.stack(masks).astype(np.float32))


def _forget_lower_bound(lbl_ref):
    l0 = lbl_ref[0:1]
    l1 = lbl_ref[1:2]
    m = jnp.maximum(l0, l1)
    e0 = jnp.exp(l0 - m)
    e1 = jnp.exp(l1 - m)
    return e0 / (e0 + e1)


def _hgrn_prompt_kernel(lbl_ref, hq_ref, hf_ref, hv_ref, mall_ref, bm_ref, o_ref, st_ref, s_scr, d_scr):
    c = HG_CHUNK
    ci = pl.program_id(1)
    levels = _hgrn_levels(c)

    @pl.when(ci == 0)
    def _():
        s_scr[...] = jnp.zeros_like(s_scr)

    lb = _forget_lower_bound(lbl_ref)
    q = hq_ref[0]
    f = lb + (1.0 - lb) * jax.nn.sigmoid(hf_ref[0])
    logf = jnp.log(f)
    k = 1.0 - f
    v = hv_ref[0]
    hi, lo = _split_bf16(logf)
    d_scr[...] = jnp.dot(mall_ref[...], jnp.concatenate([hi, lo], axis=0), preferred_element_type=F32)

    lane = lax.broadcasted_iota(jnp.int32, (c, HG_W), 1)
    head_masks = [(lane >= h * HG_D) & (lane < (h + 1) * HG_D) for h in range(HG_HEADS)]

    def by_head(x_bf):
        z = jnp.zeros_like(x_bf)
        return jnp.concatenate([jnp.where(m, x_bf, z) for m in head_masks], axis=0)

    def pair_scores(q_f32, k_f32, mask):
        p = lax.dot_general(by_head(q_f32.astype(BF16)), k_f32.astype(BF16), _NT, preferred_element_type=F32)
        return p * jnp.concatenate([mask] * HG_HEADS, axis=0)

    attn = pair_scores(q, k, bm_ref[len(levels)])
    for li in range(len(levels)):
        dq = d_scr[(2 + 2 * li) * c:(3 + 2 * li) * c, :]
        dk = d_scr[(3 + 2 * li) * c:(4 + 2 * li) * c, :]
        attn = attn + pair_scores(q * jnp.exp(dq), k * jnp.exp(dk), bm_ref[li])
    a_bf = attn.astype(BF16)
    a2 = jnp.concatenate([a_bf[h * c:(h + 1) * c] for h in range(HG_HEADS)], axis=1)
    v_bf = v.astype(BF16)
    o = jnp.dot(a2, by_head(v_bf), preferred_element_type=F32)

    st = s_scr[...]
    bcum = d_scr[0:c, :]
    o = o + lax.dot_general((q * jnp.exp(bcum)).astype(BF16), st.astype(BF16), _NT, preferred_element_type=F32)
    o_ref[0] = o

    k_dec = (k * jnp.exp(d_scr[c:2 * c, :])).astype(BF16)
    upd = lax.dot_general(v_bf, k_dec, _TN, preferred_element_type=F32)
    r = lax.broadcasted_iota(jnp.int32, (HG_W, HG_W), 0)
    cc = lax.broadcasted_iota(jnp.int32, (HG_W, HG_W), 1)
    same_head = (r // HG_D) == (cc // HG_D)
    s_new = st * jnp.exp(bcum[c - 1:c, :]) + jnp.where(same_head, upd, 0.0)
    s_scr[...] = s_new

    @pl.when(ci == pl.num_programs(1) - 1)
    def _():
        st_ref[0] = s_new


def _hgrn_prompt(rest3, lb_logits):
    b, t, _ = rest3.shape
    c = HG_CHUNK
    mall2, bm = _hgrn_constants(c)
    nr = mall2.shape[0]
    col = lambda j: (lambda bi, ci: (bi, ci, j))
    return pl.pallas_call(
        _hgrn_prompt_kernel,
        grid=(b, t // c),
        in_specs=[pl.BlockSpec((2, HG_W), lambda bi, ci: (0, 0)),
                  pl.BlockSpec((1, c, HG_W), col(2)),
                  pl.BlockSpec((1, c, HG_W), col(3)),
                  pl.BlockSpec((1, c, HG_W), col(4)),
                  pl.BlockSpec((nr, 2 * c), lambda bi, ci: (0, 0)),
                  pl.BlockSpec(bm.shape, lambda bi, ci: (0, 0, 0))],
        out_specs=[pl.BlockSpec((1, c, HG_W), lambda bi, ci: (bi, ci, 0)),
                   pl.BlockSpec((1, HG_W, HG_W), lambda bi, ci: (bi, 0, 0))],
        out_shape=[jax.ShapeDtypeStruct((b, t, HG_W), F32), jax.ShapeDtypeStruct((b, HG_W, HG_W), F32)],
        scratch_shapes=[pltpu.VMEM((HG_W, HG_W), F32), pltpu.VMEM((nr, HG_W), F32)],
        compiler_params=pltpu.CompilerParams(dimension_semantics=("arbitrary", "arbitrary"),
                                             vmem_limit_bytes=VMEM_LIMIT),
    )(lb_logits, rest3, rest3, rest3, mall2, bm)


def _silu(g):
    return g * jax.nn.sigmoid(g)


def _mix_out_kernel(x_ref, sbo_ref, hgo_ref, sbg_ref, hgg_ref, xaq_ref, xag_ref, mk_ref, mv_ref,
                    wout_ref, hgn_ref, bd2_ref, fng_ref, y_ref):
    tm = x_ref.shape[1]
    rows = max(tm, 8)

    def load(ref):
        a = ref[0]
        return jnp.broadcast_to(a, (rows, a.shape[-1])) if tm < rows else a

    x = load(x_ref)
    lane = lax.broadcasted_iota(jnp.int32, (rows, XA_W), 1)
    lane_m = lax.broadcasted_iota(jnp.int32, (mk_ref.shape[1], XA_W), 1)

    xq = (load(xaq_ref) * (1.0 / math.sqrt(XA_DH))).astype(BF16)
    zq = jnp.zeros_like(xq)
    q_st = jnp.concatenate([jnp.where((lane >= h * XA_DH) & (lane < (h + 1) * XA_DH), xq, zq)
                            for h in range(XA_HEADS)], axis=0)
    s = lax.dot_general(q_st, mk_ref[0].astype(BF16), _NT, preferred_element_type=F32)
    p = jnp.exp(s - jnp.max(s, axis=-1, keepdims=True))
    p = (p / jnp.sum(p, axis=-1, keepdims=True)).astype(BF16)
    p2 = jnp.concatenate([p[h * rows:(h + 1) * rows] for h in range(XA_HEADS)], axis=1)
    mv = mv_ref[0].astype(BF16)
    zv = jnp.zeros_like(mv)
    mv_st = jnp.concatenate([jnp.where((lane_m >= h * XA_DH) & (lane_m < (h + 1) * XA_DH), mv, zv)
                             for h in range(XA_HEADS)], axis=0)
    xa_o = jnp.dot(p2, mv_st, preferred_element_type=F32)

    hg = load(hgo_ref)
    hi, lo = _split_bf16(hg * hg)
    ms = jnp.dot(jnp.concatenate([hi, lo], axis=1), bd2_ref[...], preferred_element_type=F32) * (1.0 / HG_D)
    hg_n = (hg * lax.rsqrt(ms + EPS)) * hgn_ref[...]

    sb = (load(sbo_ref) * _silu(load(sbg_ref))).astype(BF16)
    hgm = (hg_n * _silu(load(hgg_ref))).astype(BF16)
    xam = (xa_o * _silu(load(xag_ref))).astype(BF16)
    y = x + jnp.dot(sb, wout_ref[0:SB_W, :], preferred_element_type=F32)
    y = y + jnp.dot(hgm, wout_ref[SB_W:SB_W + HG_W, :], preferred_element_type=F32)
    y = y + jnp.dot(xam, wout_ref[SB_W + HG_W:, :], preferred_element_type=F32)
    msy = jnp.mean(y * y, axis=-1, keepdims=True)
    y = (y * lax.rsqrt(msy + EPS)) * fng_ref[...]
    y_ref[0] = y[0:tm]


def _mix_out(x3, sbo3, hgo3, rest3, mk3, mv3, wout_bf, hg_norm_gain, final_gain, tm):
    b, t, d = x3.shape
    nm = mk3.shape[1]
    bd = (np.arange(HG_W)[:, None] // HG_D) == (np.arange(HG_W)[None, :] // HG_D)
    bd2 = jnp.asarray(np.concatenate([bd, bd], axis=0).astype(np.float32), dtype=BF16)
    blk = lambda w, j: pl.BlockSpec((1, tm, w), lambda bi, ti: (bi, ti, j))
    const = lambda shape: pl.BlockSpec(shape, lambda bi, ti: (0,) * len(shape))
    return pl.pallas_call(
        _mix_out_kernel,
        grid=(b, t // tm),
        in_specs=[blk(d, 0), blk(SB_W, 0), blk(HG_W, 0),
                  blk(SB_W, 0),
                  blk(HG_W, 5),
                  blk(XA_W, 6),
                  blk(XA_W, 7),
                  pl.BlockSpec((1, nm, XA_W), lambda bi, ti: (bi, 0, 0)),
                  pl.BlockSpec((1, nm, XA_W), lambda bi, ti: (bi, 0, 0)),
                  const(wout_bf.shape), const((1, HG_W)), const(bd2.shape), const((1, d))],
        out_specs=blk(d, 0),
        out_shape=jax.ShapeDtypeStruct((b, t, d), F32),
        compiler_params=pltpu.CompilerParams(dimension_semantics=("arbitrary", "arbitrary"),
                                             vmem_limit_bytes=VMEM_LIMIT),
    )(x3, sbo3, hgo3, rest3, rest3, rest3, rest3, mk3, mv3, wout_bf,
      hg_norm_gain.reshape(1, HG_W), bd2, final_gain.reshape(1, d))


def _page_matrices():
    n = PAGES_PER_STEP * SB_HEADS
    r = np.arange(n)[:, None]
    c = np.arange(n)[None, :]
    same_head = (r % SB_HEADS) == (c % SB_HEADS)
    before = same_head & ((c // SB_HEADS) < (r // SB_HEADS))
    mats = [np.concatenate([m, m], axis=1).astype(np.float32) for m in (before, same_head)]
    return jnp.asarray(np.concatenate(mats, axis=0), dtype=BF16)


def _sb_sample_kernel(pt_ref, q_ref, bias_ref, w2_ref, pm_ref, *refs):
    npg = PAGES_PER_STEP
    kt_refs, vt_refs = refs[:npg], refs[npg:2 * npg]
    o_ref, carry_ref, acc_ref = refs[2 * npg:]
    s = pl.program_id(1)
    n = npg * SB_HEADS

    @pl.when(s == 0)
    def _():
        carry_ref[...] = jnp.zeros_like(carry_ref)
        acc_ref[...] = jnp.zeros_like(acc_ref)

    row8 = lax.broadcasted_iota(jnp.int32, (SB_HEADS, SB_W), 0)
    lane8 = lax.broadcasted_iota(jnp.int32, (SB_HEADS, SB_W), 1)
    own = (lane8 // SB_DH) == row8
    qb = jnp.broadcast_to(q_ref[0].astype(F32), (SB_HEADS, SB_W))
    q_bd = jnp.where(own, qb, 0.0).astype(BF16)

    z = jnp.concatenate([jnp.dot(q_bd, kt_refs[i][0].astype(BF16), preferred_element_type=F32)
                         for i in range(npg)], axis=0)
    bias_t = jnp.concatenate([bias_ref[...]] * npg, axis=0)
    log_beta, log_1m = _log_sigmoid_pair(z + bias_t)
    hi, lo = _split_bf16(log_1m)
    r = jnp.dot(jnp.concatenate([hi, lo], axis=1), w2_ref[...], preferred_element_type=F32)
    tot = r[:, LANES:]
    thi, tlo = _split_bf16(tot)
    cross = jnp.dot(pm_ref[...], jnp.concatenate([thi, tlo], axis=0), preferred_element_type=F32)
    carry = carry_ref[...]
    suffix = r[:, :LANES] + cross[:n] + jnp.concatenate([carry] * npg, axis=0)
    carry_ref[...] = carry + cross[n:n + SB_HEADS]
    a = jnp.exp(log_beta + suffix).astype(BF16)
    acc = acc_ref[...]
    for i in range(npg):
        acc = acc + lax.dot_general(a[i * SB_HEADS:(i + 1) * SB_HEADS], vt_refs[i][0].astype(BF16), _NT,
                                    preferred_element_type=F32)
    acc_ref[...] = acc

    @pl.when(s == pl.num_programs(1) - 1)
    def _():
        o_ref[0] = jnp.sum(jnp.where(own, acc, 0.0), axis=0, keepdims=True)


def _sb_sample(q_bf, bias, page_table, ckt, cvt):
    b = q_bf.shape[0]
    n_pages = page_table.shape[1]
    page = ckt.shape[2]
    assert page == LANES and n_pages % PAGES_PER_STEP == 0
    npg = PAGES_PER_STEP
    nsteps = n_pages // npg

    def page_spec(i):
        return pl.BlockSpec((1, SB_W, page),
                            lambda bi, si, pt: (pt[bi * n_pages + (n_pages - 1 - (si * npg + i))], 0, 0))

    n = npg * SB_HEADS
    grid_spec = pltpu.PrefetchScalarGridSpec(
        num_scalar_prefetch=1,
        grid=(b, nsteps),
        in_specs=[pl.BlockSpec((1, 1, SB_W), lambda bi, si, pt: (bi, 0, 0)),
                  pl.BlockSpec((SB_HEADS, LANES), lambda bi, si, pt: (0, 0)),
                  pl.BlockSpec((2 * LANES, 2 * LANES), lambda bi, si, pt: (0, 0)),
                  pl.BlockSpec((2 * n, 2 * n), lambda bi, si, pt: (0, 0))]
                 + [page_spec(i) for i in range(npg)] * 2,
        out_specs=pl.BlockSpec((1, 1, SB_W), lambda bi, si, pt: (bi, 0, 0)),
        scratch_shapes=[pltpu.VMEM((SB_HEADS, LANES), F32), pltpu.VMEM((SB_HEADS, SB_W), F32)])
    bias_rep = jnp.broadcast_to(bias.astype(F32)[:, None], (SB_HEADS, LANES))
    out = pl.pallas_call(
        _sb_sample_kernel,
        grid_spec=grid_spec,
        out_shape=jax.ShapeDtypeStruct((b, 1, SB_W), F32),
        compiler_params=pltpu.CompilerParams(dimension_semantics=("arbitrary", "arbitrary"),
                                             vmem_limit_bytes=VMEM_LIMIT),
    )(page_table.reshape(-1), q_bf.reshape(b, 1, SB_W), bias_rep, _suffix_matrix(), _page_matrices(),
      *([ckt] * npg), *([cvt] * npg))
    return out.reshape(b, SB_W)


def _hgrn_sample_kernel(lbl_ref, s_ref, q_ref, f_ref, v_ref, o_ref, sn_ref):
    l0 = lbl_ref[0]
    l1 = lbl_ref[1]
    m = jnp.maximum(l0, l1)
    e0 = jnp.exp(l0 - m)
    e1 = jnp.exp(l1 - m)
    lb = e0 / (e0 + e1)
    f = lb + (1.0 - lb) * jax.nn.sigmoid(f_ref[0])
    k = 1.0 - f
    v = v_ref[0]
    v_rows = jnp.concatenate([jnp.broadcast_to(v[:, h * HG_D:(h + 1) * HG_D], (HG_D, HG_D))
                              for h in range(HG_HEADS)], axis=0)
    s_new = f * s_ref[0] + k * v_rows
    sn_ref[0] = s_new
    qs = q_ref[0] * s_new
    o_ref[0] = jnp.sum(qs.reshape(HG_HEADS, HG_D, HG_D), axis=1)


def _hgrn_sample(state, q_col, f_col, v_row, lb_logits):
    b = state.shape[0]
    row = lambda shape: pl.BlockSpec((1,) + shape, lambda bi: (bi, 0, 0))
    return pl.pallas_call(
        _hgrn_sample_kernel,
        grid=(b,),
        in_specs=[pl.BlockSpec((2, HG_W, 1), lambda bi: (0, 0, 0)),
                  row((HG_W, HG_D)), row((HG_W, 1)), row((HG_W, 1)), row((1, HG_W))],
        out_specs=[row((HG_HEADS, HG_D)), row((HG_W, HG_D))],
        out_shape=[jax.ShapeDtypeStruct((b, HG_HEADS, HG_D), F32), jax.ShapeDtypeStruct((b, HG_W, HG_D), F32)],
        compiler_params=pltpu.CompilerParams(dimension_semantics=("arbitrary",)),
    )(lb_logits.reshape(2, HG_W, 1), state, q_col, f_col, v_row)


def _in_proj_outs():
    qs = 1.0 / math.sqrt(SB_DH)
    return [(0, SB_W, BF16, qs),
            (SB_W, 2 * SB_W, BF16, 1.0),
            (2 * SB_W, 3 * SB_W, BF16, 1.0),
            (SB_W, 2 * SB_W, F32, 1.0),
            (2 * SB_W, 3 * SB_W, F32, 1.0),
            (3 * SB_W, 3 * SB_W + D_REST, F32, 1.0)]


def kernel(x_prompt, x_sample, mem_prompt, cache_k, cache_v, page_table, state_hgrn, cache_mem_k, cache_mem_v,
           norm_gain, w_in, sb_bias, hg_lb_logits, hg_norm_gain, mem_norm_gain, w_mem_kv, w_out, final_norm_gain):
    batch, seq, d = x_prompt.shape
    dec_b = x_sample.shape[0]
    n_mem = mem_prompt.shape[1]
    depth = w_in.shape[0]
    assert depth == 1 and x_sample.shape[1] == 1
    l = 0
    lb_logits = hg_lb_logits.astype(F32)
    assert lb_logits.shape[0] == 2
    w_in_bf = w_in[l].astype(BF16)
    w_out_bf = w_out[l].astype(BF16)
    w_mem_bf = w_mem_kv[l].astype(BF16)
    bias = sb_bias[l].astype(F32)

    q_bf, k_bf, v_bf, k_p, v_p, rest = _norm_proj(x_prompt.reshape(batch * seq, d), norm_gain[l], w_in_bf,
                                                  _in_proj_outs(), tm=256)
    mk, mv = _norm_proj(mem_prompt.reshape(batch * n_mem, d), mem_norm_gain[l], w_mem_bf,
                        [(0, XA_W, F32, 1.0), (XA_W, 2 * XA_W, F32, 1.0)], tm=256)
    sb_o = _sb_prompt(q_bf, k_bf, v_bf, bias, batch, seq)
    rest3 = rest.reshape(batch, seq, D_REST)
    hg_o, st_t = _hgrn_prompt(rest3, lb_logits)
    y_prompt = _mix_out(x_prompt, sb_o.reshape(batch, seq, SB_W), hg_o, rest3,
                        mk.reshape(batch, n_mem, XA_W), mv.reshape(batch, n_mem, XA_W),
                        w_out_bf, hg_norm_gain[l], final_norm_gain, tm=256)
    hgrn_prompt = jnp.stack([st_t[:, h * HG_D:(h + 1) * HG_D, h * HG_D:(h + 1) * HG_D]
                             for h in range(HG_HEADS)], axis=1).swapaxes(-1, -2)

    qs_bf, _, _, k_s, v_s, rest_s = _norm_proj(x_sample.reshape(dec_b, d), norm_gain[l], w_in_bf,
                                               _in_proj_outs(), tm=dec_b)
    n_pool, page = cache_k.shape[1], cache_k.shape[2]
    ckt = jnp.transpose(cache_k[l], (0, 2, 3, 1)).reshape(n_pool, SB_W, page)
    cvt = jnp.transpose(cache_v[l], (0, 2, 3, 1)).reshape(n_pool, SB_W, page)
    sb_o_s = _sb_sample(qs_bf, bias, page_table, ckt, cvt)
    hq_s = rest_s[:, SB_W:SB_W + HG_W].reshape(dec_b, HG_W, 1)
    hf_s = rest_s[:, SB_W + HG_W:SB_W + 2 * HG_W].reshape(dec_b, HG_W, 1)
    hv_s = rest_s[:, SB_W + 2 * HG_W:SB_W + 3 * HG_W].reshape(dec_b, 1, HG_W)
    hg_o_s, st_s = _hgrn_sample(state_hgrn[l].astype(F32).reshape(dec_b, HG_W, HG_D), hq_s, hf_s, hv_s, lb_logits)
    y_sample = _mix_out(x_sample, sb_o_s.reshape(dec_b, 1, SB_W), hg_o_s.reshape(dec_b, 1, HG_W),
                        rest_s.reshape(dec_b, 1, D_REST),
                        cache_mem_k[l].reshape(dec_b, n_mem, XA_W), cache_mem_v[l].reshape(dec_b, n_mem, XA_W),
                        w_out_bf, hg_norm_gain[l], final_norm_gain, tm=1)

    shp = (1, batch, seq, SB_HEADS, SB_DH)
    return (y_prompt, y_sample, k_p.reshape(shp), v_p.reshape(shp),
            hgrn_prompt.reshape(1, batch, HG_HEADS, HG_D, HG_D),
            mk.reshape(1, batch, n_mem, XA_HEADS, XA_DH), mv.reshape(1, batch, n_mem, XA_HEADS, XA_DH),
            k_s.reshape(1, dec_b, 1, SB_HEADS, SB_DH), v_s.reshape(1, dec_b, 1, SB_HEADS, SB_DH),
            st_s.reshape(1, dec_b, HG_HEADS, HG_D, HG_D).astype(state_hgrn.dtype))
```

```python
import functools
import math

import numpy as np
import jax
import jax.numpy as jnp
from jax import lax
from jax.experimental import pallas as pl
from jax.experimental.pallas import tpu as pltpu

F32 = jnp.float32
BF16 = jnp.bfloat16
EPS = 1e-6

SB_HEADS, SB_DH = 8, 64
HG_HEADS, HG_D = 4, 64
XA_HEADS, XA_DH = 4, 64
SB_W = SB_HEADS * SB_DH
HG_W = HG_HEADS * HG_D
XA_W = XA_HEADS * XA_DH
D_REST = SB_W + 3 * HG_W + HG_W + 2 * XA_W

LANES = 128
SB_TQ = 512
SB_TK = 128
NEG_BIG = -1e30
LOG2E = math.log2(math.e)
HG_CHUNK = 128
PAGES_PER_STEP = 16
VMEM_LIMIT = 48 * 1024 * 1024

_NT = (((1,), (1,)), ((), ()))
_TN = (((0,), (0,)), ((), ()))


def _split_bf16(x):
    hi = x.astype(BF16)
    lo = (x - hi.astype(F32)).astype(BF16)
    return hi, lo


def _log_sigmoid_pair(t):
    sp = jnp.log1p(jnp.exp(-jnp.abs(t)))
    log_beta = jnp.minimum(t, 0.0) - sp
    return log_beta, log_beta - t


def _norm_proj_kernel(x_ref, g_ref, w_ref, *refs, outs, t_widths):
    wt_ref, out_refs = (refs[0], refs[1:]) if t_widths else (None, refs)
    x = x_ref[...]
    ms = jnp.mean(x * x, axis=-1, keepdims=True)
    xn = ((x * lax.rsqrt(ms + EPS)) * g_ref[...]).astype(BF16)
    cache = {}
    for o_ref, (lo, hi, scale) in zip(out_refs, outs):
        if (lo, hi) not in cache:
            cache[(lo, hi)] = jnp.dot(xn, w_ref[:, lo:hi], preferred_element_type=F32)
        h = cache[(lo, hi)]
        if scale != 1.0:
            h = h * scale
        o_ref[...] = h.astype(o_ref.dtype)
    off = 0
    for o_ref, width in zip(out_refs[len(outs):], t_widths):
        o_ref[0] = lax.dot_general(wt_ref[off:off + width, :], xn, _NT, preferred_element_type=F32)
        off += width


def _norm_proj(x3, gain, w_bf, outs, tm, wt_bf=None, t_widths=()):
    b, t, d = x3.shape
    m = b * t
    n = w_bf.shape[1]
    tm = min(tm, m)
    assert (t % tm == 0 or tm == m) and m % tm == 0
    nbt = max(t // tm, 1)
    assert (wt_bf is None) == (len(t_widths) == 0)
    t_args = [] if wt_bf is None else [wt_bf]
    kern = functools.partial(_norm_proj_kernel, outs=[(lo, hi, sc) for lo, hi, _, sc in outs], t_widths=tuple(t_widths))
    return pl.pallas_call(
        kern,
        grid=(m // tm,),
        in_specs=[pl.BlockSpec((tm, d), lambda i: (i, 0)),
                  pl.BlockSpec((1, d), lambda i: (0, 0)),
                  pl.BlockSpec((d, n), lambda i: (0, 0))]
                 + [pl.BlockSpec(a.shape, lambda i: (0, 0)) for a in t_args],
        out_specs=[pl.BlockSpec((tm, hi - lo), lambda i: (i, 0)) for lo, hi, _, _ in outs]
                  + [pl.BlockSpec((1, w, tm), lambda i: (i // nbt, 0, i % nbt)) for w in t_widths],
        out_shape=[jax.ShapeDtypeStruct((m, hi - lo), dt) for lo, hi, dt, _ in outs]
                  + [jax.ShapeDtypeStruct((b, w, t), F32) for w in t_widths],
        compiler_params=pltpu.CompilerParams(dimension_semantics=("arbitrary",), vmem_limit_bytes=VMEM_LIMIT),
    )(x3.reshape(m, d), gain.reshape(1, d), w_bf, *t_args)


def _suffix_matrix():
    jp = np.arange(LANES)[:, None]
    j = np.arange(LANES)[None, :]
    w = np.concatenate([(jp > j).astype(np.float32), np.ones((LANES, LANES), np.float32)], axis=1)
    return jnp.asarray(np.concatenate([w, w], axis=0), dtype=BF16)


def _sb_prompt_kernel(bias_ref, q_ref, kt_ref, vt_ref, w2_ref, o_ref, kbd_ref, vbd_ref, carry_ref, acc_ref, lb_scr, x_scr):
    tq, tk = SB_TQ, SB_TK
    hp = pl.program_id(1)
    qi = pl.program_id(2)
    nkb = kt_ref.shape[2] // tk

    @pl.when(qi == 0)
    def _():
        row_v = lax.broadcasted_iota(jnp.int32, (LANES, tk), 0)
        for j in range(nkb):
            for src, dst in ((kt_ref, kbd_ref), (vt_ref, vbd_ref)):
                blk = src[0, :, j * tk:(j + 1) * tk].astype(BF16)
                zero = jnp.zeros_like(blk)
                dst[j, :, 0:tk] = jnp.where(row_v < SB_DH, blk, zero)
                dst[j, :, tk:2 * tk] = jnp.where(row_v >= SB_DH, blk, zero)

    q2 = q_ref[...]
    b2 = (bias_ref[2 * hp] * LOG2E, bias_ref[2 * hp + 1] * LOG2E)
    qpos = qi * tq + lax.broadcasted_iota(jnp.int32, (tq, LANES), 0)
    col = lax.broadcasted_iota(jnp.int32, (tq, LANES), 1)

    carry_ref[...] = jnp.zeros_like(carry_ref)
    acc_ref[...] = jnp.zeros_like(acc_ref)
    grp = tq // tk

    def block_of(m, g):
        return (qi - m) * grp + (grp - 1 - g)

    def finish_scores(z, j, slot, g, masked):
        causal = (j * tk + col) < qpos if masked else None
        for half in range(2):
            rows = slice(half * tq, (half + 1) * tq)
            t = z[:, half * tk:(half + 1) * tk] * LOG2E + b2[half]
            sp = jnp.log(1.0 + jnp.exp2(-jnp.abs(t))) * LOG2E
            lb = jnp.minimum(t, 0.0) - sp
            l1m = lb - t
            if masked:
                l1m = jnp.where(causal, l1m, 0.0)
                lb = jnp.where(causal, lb, NEG_BIG)
            hi, lo = _split_bf16(l1m)
            lb_scr[slot, g, rows, :] = lb
            x_scr[slot, g, rows, 0:LANES] = hi
            x_scr[slot, g, rows, LANES:2 * LANES] = lo

    def finish_weights(r, j, slot, g):
        suffix = r[:, :LANES] + carry_ref[...]
        carry_ref[...] += r[:, LANES:]
        a = jnp.exp2(lb_scr[slot, g] + suffix).astype(BF16)
        a2 = jnp.concatenate([a[:tq], a[tq:]], axis=1)
        acc_ref[...] += lax.dot_general(a2, vbd_ref[j], _NT, preferred_element_type=F32)

    def stage(m_w, slot_w, m_s, slot_s, masked=False):
        rs, zs = [], []
        for g in range(grp):
            if m_w is not None:
                rs.append(jnp.dot(x_scr[slot_w, g], w2_ref[...], preferred_element_type=F32))
            if m_s is not None:
                zs.append(jnp.dot(q2, kbd_ref[block_of(m_s, g)], preferred_element_type=F32))
        for g in range(grp):
            if m_w is not None:
                finish_weights(rs[g], block_of(m_w, g), slot_w, g)
        for g in range(grp):
            if m_s is not None:
                finish_scores(zs[g], block_of(m_s, g), slot_s, g, masked)

    stage(None, None, 0, 0, masked=True)

    def loop_body(p, c):
        stage(2 * p, 0, 2 * p + 1, 1)
        stage(2 * p + 1, 1, 2 * p + 2, 0)
        return c

    lax.fori_loop(0, qi // 2, loop_body, 0)

    @pl.when(qi % 2 == 1)
    def _():
        stage(qi - 1, 0, qi, 1)
        stage(qi, 1, None, None)

    @pl.when(qi % 2 == 0)
    def _():
        stage(qi, 0, None, None)

    o_ref[...] = acc_ref[...]


def _sb_prompt(q_bf, kt, vt, bias):
    tq, tk = SB_TQ, SB_TK
    batch, _, seq = kt.shape
    assert seq % tq == 0 and tq % tk == 0
    nq = seq // tq
    npairs = SB_W // LANES
    return pl.pallas_call(
        _sb_prompt_kernel,
        grid=(batch, npairs, nq),
        in_specs=[pl.BlockSpec(memory_space=pltpu.SMEM),
                  pl.BlockSpec((tq, LANES), lambda b, h, i: (b * nq + i, h)),
                  pl.BlockSpec((1, LANES, seq), lambda b, h, i: (b, h, 0)),
                  pl.BlockSpec((1, LANES, seq), lambda b, h, i: (b, h, 0)),
                  pl.BlockSpec((2 * LANES, 2 * LANES), lambda b, h, i: (0, 0))],
        out_specs=pl.BlockSpec((tq, LANES), lambda b, h, i: (b * nq + i, h)),
        out_shape=jax.ShapeDtypeStruct((batch * seq, SB_W), F32),
        scratch_shapes=[pltpu.VMEM((seq // tk, LANES, 2 * tk), BF16),
                        pltpu.VMEM((seq // tk, LANES, 2 * tk), BF16),
                        pltpu.VMEM((2 * tq, LANES), F32),
                        pltpu.VMEM((tq, LANES), F32),
                        pltpu.VMEM((2, tq // tk, 2 * tq, LANES), F32),
                        pltpu.VMEM((2, tq // tk, 2 * tq, 2 * LANES), BF16)],
        compiler_params=pltpu.CompilerParams(dimension_semantics=("arbitrary", "arbitrary", "arbitrary"),
                                             vmem_limit_bytes=VMEM_LIMIT),
    )(bias, q_bf, kt, vt, _suffix_matrix())


def _hgrn_levels(c):
    return [c >> (i + 1) for i in range(int(math.log2(c)))]


def _hgrn_constants(c):
    t = np.arange(c)[:, None]
    u = np.arange(c)[None, :]
    rows = [(u <= t), (u > t)]
    masks = []
    for hs in _hgrn_levels(c):
        mid = (t // (2 * hs)) * (2 * hs) + hs - 1
        upper = (t % (2 * hs)) >= hs
        rows.append(upper & (u > mid) & (u <= t))
        rows.append((~upper) & (u > t) & (u <= mid))
        s = u
        masks.append(((t // (2 * hs)) == (s // (2 * hs))) & upper & ((s % (2 * hs)) < hs))
    masks.append(t == u)
    mall = np.concatenate(rows, axis=0).astype(np.float32)
    mall2 = np.concatenate([mall, mall], axis=1)
    return jnp.asarray(mall2, dtype=BF16), jnp.asarray(np.stack(masks).astype(np.float32))


def _forget_lower_bound(lbl_ref):
    l0 = lbl_ref[0:1]
    l1 = lbl_ref[1:2]
    m = jnp.maximum(l0, l1)
    e0 = jnp.exp(l0 - m)
    e1 = jnp.exp(l1 - m)
    return e0 / (e0 + e1)


def _hgrn_prompt_kernel(lbl_ref, hq_ref, hf_ref, hv_ref, mall_ref, bm_ref, o_ref, st_ref, s_scr, d_scr):
    c = HG_CHUNK
    ci = pl.program_id(1)
    levels = _hgrn_levels(c)

    @pl.when(ci == 0)
    def _():
        s_scr[...] = jnp.zeros_like(s_scr)

    lb = _forget_lower_bound(lbl_ref)
    q = hq_ref[0]
    f = lb + (1.0 - lb) * jax.nn.sigmoid(hf_ref[0])
    logf = jnp.log(f)
    k = 1.0 - f
    v = hv_ref[0]
    hi, lo = _split_bf16(logf)
    d_scr[...] = jnp.dot(mall_ref[...], jnp.concatenate([hi, lo], axis=0), preferred_element_type=F32)

    lane = lax.broadcasted_iota(jnp.int32, (c, HG_W), 1)
    head_masks = [(lane >= h * HG_D) & (lane < (h + 1) * HG_D) for h in range(HG_HEADS)]

    def by_head(x_bf):
        z = jnp.zeros_like(x_bf)
        return jnp.concatenate([jnp.where(m, x_bf, z) for m in head_masks], axis=0)

    def pair_scores(q_f32, k_f32, mask):
        p = lax.dot_general(by_head(q_f32.astype(BF16)), k_f32.astype(BF16), _NT, preferred_element_type=F32)
        return p * jnp.concatenate([mask] * HG_HEADS, axis=0)

    attn = pair_scores(q, k, bm_ref[len(levels)])
    for li in range(len(levels)):
        dq = d_scr[(2 + 2 * li) * c:(3 + 2 * li) * c, :]
        dk = d_scr[(3 + 2 * li) * c:(4 + 2 * li) * c, :]
        attn = attn + pair_scores(q * jnp.exp(dq), k * jnp.exp(dk), bm_ref[li])
    a_bf = attn.astype(BF16)
    a2 = jnp.concatenate([a_bf[h * c:(h + 1) * c] for h in range(HG_HEADS)], axis=1)
    v_bf = v.astype(BF16)
    o = jnp.dot(a2, by_head(v_bf), preferred_element_type=F32)

    st = s_scr[...]
    bcum = d_scr[0:c, :]
    o = o + lax.dot_general((q * jnp.exp(bcum)).astype(BF16), st.astype(BF16), _NT, preferred_element_type=F32)
    o_ref[0] = o

    k_dec = (k * jnp.exp(d_scr[c:2 * c, :])).astype(BF16)
    upd = lax.dot_general(v_bf, k_dec, _TN, preferred_element_type=F32)
    r = lax.broadcasted_iota(jnp.int32, (HG_W, HG_W), 0)
    cc = lax.broadcasted_iota(jnp.int32, (HG_W, HG_W), 1)
    same_head = (r // HG_D) == (cc // HG_D)
    s_new = st * jnp.exp(bcum[c - 1:c, :]) + jnp.where(same_head, upd, 0.0)
    s_scr[...] = s_new

    @pl.when(ci == pl.num_programs(1) - 1)
    def _():
        st_ref[0] = s_new


def _hgrn_prompt(rest3, lb_logits):
    b, t, _ = rest3.shape
    c = HG_CHUNK
    mall2, bm = _hgrn_constants(c)
    nr = mall2.shape[0]
    col = lambda j: (lambda bi, ci: (bi, ci, j))
    return pl.pallas_call(
        _hgrn_prompt_kernel,
        grid=(b, t // c),
        in_specs=[pl.BlockSpec((2, HG_W), lambda bi, ci: (0, 0)),
                  pl.BlockSpec((1, c, HG_W), col(2)),
                  pl.BlockSpec((1, c, HG_W), col(3)),
                  pl.BlockSpec((1, c, HG_W), col(4)),
                  pl.BlockSpec((nr, 2 * c), lambda bi, ci: (0, 0)),
                  pl.BlockSpec(bm.shape, lambda bi, ci: (0, 0, 0))],
        out_specs=[pl.BlockSpec((1, c, HG_W), lambda bi, ci: (bi, ci, 0)),
                   pl.BlockSpec((1, HG_W, HG_W), lambda bi, ci: (bi, 0, 0))],
        out_shape=[jax.ShapeDtypeStruct((b, t, HG_W), F32), jax.ShapeDtypeStruct((b, HG_W, HG_W), F32)],
        scratch_shapes=[pltpu.VMEM((HG_W, HG_W), F32), pltpu.VMEM((nr, HG_W), F32)],
        compiler_params=pltpu.CompilerParams(dimension_semantics=("arbitrary", "arbitrary"),
                                             vmem_limit_bytes=VMEM_LIMIT),
    )(lb_logits, rest3, rest3, rest3, mall2, bm)


def _silu(g):
    return g * jax.nn.sigmoid(g)


def _mix_out_kernel(x_ref, sbo_ref, hgo_ref, sbg_ref, hgg_ref, xaq_ref, xag_ref, mkt_ref, mvt_ref,
                    wout_ref, hgn_ref, bd2_ref, fng_ref, y_ref):
    tm = x_ref.shape[1]
    rows = max(tm, 8)

    def load(ref):
        a = ref[0]
        return jnp.broadcast_to(a, (rows, a.shape[-1])) if tm < rows else a

    x = load(x_ref)
    lane = lax.broadcasted_iota(jnp.int32, (rows, XA_W), 1)
    row_m = lax.broadcasted_iota(jnp.int32, (XA_W, mkt_ref.shape[2]), 0)

    xq = (load(xaq_ref) * (1.0 / math.sqrt(XA_DH))).astype(BF16)
    zq = jnp.zeros_like(xq)
    q_st = jnp.concatenate([jnp.where((lane >= h * XA_DH) & (lane < (h + 1) * XA_DH), xq, zq)
                            for h in range(XA_HEADS)], axis=0)
    s = jnp.dot(q_st, mkt_ref[0].astype(BF16), preferred_element_type=F32)
    p = jnp.exp(s - jnp.max(s, axis=-1, keepdims=True))
    p = (p / jnp.sum(p, axis=-1, keepdims=True)).astype(BF16)
    p2 = jnp.concatenate([p[h * rows:(h + 1) * rows] for h in range(XA_HEADS)], axis=1)
    mvt = mvt_ref[0].astype(BF16)
    zv = jnp.zeros_like(mvt)
    mvt_st = jnp.concatenate([jnp.where((row_m >= h * XA_DH) & (row_m < (h + 1) * XA_DH), mvt, zv)
                              for h in range(XA_HEADS)], axis=1)
    xa_o = lax.dot_general(p2, mvt_st, _NT, preferred_element_type=F32)

    hg = load(hgo_ref)
    hi, lo = _split_bf16(hg * hg)
    ms = jnp.dot(jnp.concatenate([hi, lo], axis=1), bd2_ref[...], preferred_element_type=F32) * (1.0 / HG_D)
    hg_n = (hg * lax.rsqrt(ms + EPS)) * hgn_ref[...]

    sb = (load(sbo_ref) * _silu(load(sbg_ref))).astype(BF16)
    hgm = (hg_n * _silu(load(hgg_ref))).astype(BF16)
    xam = (xa_o * _silu(load(xag_ref))).astype(BF16)
    y = x + jnp.dot(sb, wout_ref[0:SB_W, :], preferred_element_type=F32)
    y = y + jnp.dot(hgm, wout_ref[SB_W:SB_W + HG_W, :], preferred_element_type=F32)
    y = y + jnp.dot(xam, wout_ref[SB_W + HG_W:, :], preferred_element_type=F32)
    msy = jnp.mean(y * y, axis=-1, keepdims=True)
    y = (y * lax.rsqrt(msy + EPS)) * fng_ref[...]
    y_ref[0] = y[0:tm]


def _mix_out(x3, sbo3, hgo3, rest3, mkt3, mvt3, wout_bf, hg_norm_gain, final_gain, tm):
    b, t, d = x3.shape
    nm = mkt3.shape[2]
    bd = (np.arange(HG_W)[:, None] // HG_D) == (np.arange(HG_W)[None, :] // HG_D)
    bd2 = jnp.asarray(np.concatenate([bd, bd], axis=0).astype(np.float32), dtype=BF16)
    blk = lambda w, j: pl.BlockSpec((1, tm, w), lambda bi, ti: (bi, ti, j))
    const = lambda shape: pl.BlockSpec(shape, lambda bi, ti: (0,) * len(shape))
    return pl.pallas_call(
        _mix_out_kernel,
        grid=(b, t // tm),
        in_specs=[blk(d, 0), blk(SB_W, 0), blk(HG_W, 0),
                  blk(SB_W, 0),
                  blk(HG_W, 5),
                  blk(XA_W, 6),
                  blk(XA_W, 7),
                  pl.BlockSpec((1, XA_W, nm), lambda bi, ti: (bi, 0, 0)),
                  pl.BlockSpec((1, XA_W, nm), lambda bi, ti: (bi, 0, 0)),
                  const(wout_bf.shape), const((1, HG_W)), const(bd2.shape), const((1, d))],
        out_specs=blk(d, 0),
        out_shape=jax.ShapeDtypeStruct((b, t, d), F32),
        compiler_params=pltpu.CompilerParams(dimension_semantics=("arbitrary", "arbitrary"),
                                             vmem_limit_bytes=VMEM_LIMIT),
    )(x3, sbo3, hgo3, rest3, rest3, rest3, rest3, mkt3, mvt3, wout_bf,
      hg_norm_gain.reshape(1, HG_W), bd2, final_gain.reshape(1, d))


def _page_matrices():
    n = PAGES_PER_STEP * SB_HEADS
    r = np.arange(n)[:, None]
    c = np.arange(n)[None, :]
    same_head = (r % SB_HEADS) == (c % SB_HEADS)
    before = same_head & ((c // SB_HEADS) < (r // SB_HEADS))
    mats = [np.concatenate([m, m], axis=1).astype(np.float32) for m in (before, same_head)]
    return jnp.asarray(np.concatenate(mats, axis=0), dtype=BF16)


def _sb_sample_kernel(pt_ref, q_ref, bias_ref, w2_ref, pm_ref, *refs):
    npg = PAGES_PER_STEP
    kt_refs, vt_refs = refs[:npg], refs[npg:2 * npg]
    o_ref, carry_ref, acc_ref = refs[2 * npg:]
    s = pl.program_id(1)
    n = npg * SB_HEADS

    @pl.when(s == 0)
    def _():
        carry_ref[...] = jnp.zeros_like(carry_ref)
        acc_ref[...] = jnp.zeros_like(acc_ref)

    row8 = lax.broadcasted_iota(jnp.int32, (SB_HEADS, SB_W), 0)
    lane8 = lax.broadcasted_iota(jnp.int32, (SB_HEADS, SB_W), 1)
    own = (lane8 // SB_DH) == row8
    qb = jnp.broadcast_to(q_ref[0].astype(F32), (SB_HEADS, SB_W))
    q_bd = jnp.where(own, qb, 0.0).astype(BF16)

    z = jnp.concatenate([jnp.dot(q_bd, kt_refs[i][0].astype(BF16), preferred_element_type=F32)
                         for i in range(npg)], axis=0)
    bias_t = jnp.concatenate([bias_ref[...]] * npg, axis=0)
    log_beta, log_1m = _log_sigmoid_pair(z + bias_t)
    hi, lo = _split_bf16(log_1m)
    r = jnp.dot(jnp.concatenate([hi, lo], axis=1), w2_ref[...], preferred_element_type=F32)
    tot = r[:, LANES:]
    thi, tlo = _split_bf16(tot)
    cross = jnp.dot(pm_ref[...], jnp.concatenate([thi, tlo], axis=0), preferred_element_type=F32)
    carry = carry_ref[...]
    suffix = r[:, :LANES] + cross[:n] + jnp.concatenate([carry] * npg, axis=0)
    carry_ref[...] = carry + cross[n:n + SB_HEADS]
    a = jnp.exp(log_beta + suffix).astype(BF16)
    acc = acc_ref[...]
    for i in range(npg):
        acc = acc + lax.dot_general(a[i * SB_HEADS:(i + 1) * SB_HEADS], vt_refs[i][0].astype(BF16), _NT,
                                    preferred_element_type=F32)
    acc_ref[...] = acc

    @pl.when(s == pl.num_programs(1) - 1)
    def _():
        o_ref[0] = jnp.sum(jnp.where(own, acc, 0.0), axis=0, keepdims=True)


def _sb_sample(q_bf, bias, page_table, ckt, cvt):
    b = q_bf.shape[0]
    n_pages = page_table.shape[1]
    page = ckt.shape[2]
    assert page == LANES and n_pages % PAGES_PER_STEP == 0
    npg = PAGES_PER_STEP
    nsteps = n_pages // npg

    def page_spec(i):
        return pl.BlockSpec((1, SB_W, page),
                            lambda bi, si, pt: (pt[bi * n_pages + (n_pages - 1 - (si * npg + i))], 0, 0))

    n = npg * SB_HEADS
    grid_spec = pltpu.PrefetchScalarGridSpec(
        num_scalar_prefetch=1,
        grid=(b, nsteps),
        in_specs=[pl.BlockSpec((1, 1, SB_W), lambda bi, si, pt: (bi, 0, 0)),
                  pl.BlockSpec((SB_HEADS, LANES), lambda bi, si, pt: (0, 0)),
                  pl.BlockSpec((2 * LANES, 2 * LANES), lambda bi, si, pt: (0, 0)),
                  pl.BlockSpec((2 * n, 2 * n), lambda bi, si, pt: (0, 0))]
                 + [page_spec(i) for i in range(npg)] * 2,
        out_specs=pl.BlockSpec((1, 1, SB_W), lambda bi, si, pt: (bi, 0, 0)),
        scratch_shapes=[pltpu.VMEM((SB_HEADS, LANES), F32), pltpu.VMEM((SB_HEADS, SB_W), F32)])
    bias_rep = jnp.broadcast_to(bias.astype(F32)[:, None], (SB_HEADS, LANES))
    out = pl.pallas_call(
        _sb_sample_kernel,
        grid_spec=grid_spec,
        out_shape=jax.ShapeDtypeStruct((b, 1, SB_W), F32),
        compiler_params=pltpu.CompilerParams(dimension_semantics=("arbitrary", "arbitrary"),
                                             vmem_limit_bytes=VMEM_LIMIT),
    )(page_table.reshape(-1), q_bf.reshape(b, 1, SB_W), bias_rep, _suffix_matrix(), _page_matrices(),
      *([ckt] * npg), *([cvt] * npg))
    return out.reshape(b, SB_W)


def _hgrn_sample_kernel(lbl_ref, s_ref, q_ref, f_ref, v_ref, o_ref, sn_ref):
    l0 = lbl_ref[0]
    l1 = lbl_ref[1]
    m = jnp.maximum(l0, l1)
    e0 = jnp.exp(l0 - m)
    e1 = jnp.exp(l1 - m)
    lb = e0 / (e0 + e1)
    f = lb + (1.0 - lb) * jax.nn.sigmoid(f_ref[0])
    k = 1.0 - f
    v = v_ref[0]
    v_rows = jnp.concatenate([jnp.broadcast_to(v[:, h * HG_D:(h + 1) * HG_D], (HG_D, HG_D))
                              for h in range(HG_HEADS)], axis=0)
    s_new = f * s_ref[0] + k * v_rows
    sn_ref[0] = s_new
    qs = q_ref[0] * s_new
    o_ref[0] = jnp.sum(qs.reshape(HG_HEADS, HG_D, HG_D), axis=1)


def _hgrn_sample(state, q_col, f_col, v_row, lb_logits):
    b = state.shape[0]
    row = lambda shape: pl.BlockSpec((1,) + shape, lambda bi: (bi, 0, 0))
    return pl.pallas_call(
        _hgrn_sample_kernel,
        grid=(b,),
        in_specs=[pl.BlockSpec((2, HG_W, 1), lambda bi: (0, 0, 0)),
                  row((HG_W, HG_D)), row((HG_W, 1)), row((HG_W, 1)), row((1, HG_W))],
        out_specs=[row((HG_HEADS, HG_D)), row((HG_W, HG_D))],
        out_shape=[jax.ShapeDtypeStruct((b, HG_HEADS, HG_D), F32), jax.ShapeDtypeStruct((b, HG_W, HG_D), F32)],
        compiler_params=pltpu.CompilerParams(dimension_semantics=("arbitrary",)),
    )(lb_logits.reshape(2, HG_W, 1), state, q_col, f_col, v_row)


def _in_proj_outs(transposed_kv):
    qs = 1.0 / math.sqrt(SB_DH)
    q_out = (0, SB_W, BF16, qs)
    rest_out = (3 * SB_W, 3 * SB_W + D_REST, F32, 1.0)
    if transposed_kv:
        return [q_out, rest_out]
    return [q_out, (SB_W, 2 * SB_W, F32, 1.0), (2 * SB_W, 3 * SB_W, F32, 1.0), rest_out]


def kernel(x_prompt, x_sample, mem_prompt, cache_k, cache_v, page_table, state_hgrn, cache_mem_k, cache_mem_v,
           norm_gain, w_in, sb_bias, hg_lb_logits, hg_norm_gain, mem_norm_gain, w_mem_kv, w_out, final_norm_gain):
    batch, seq, d = x_prompt.shape
    dec_b = x_sample.shape[0]
    n_mem = mem_prompt.shape[1]
    depth = w_in.shape[0]
    assert depth == 1 and x_sample.shape[1] == 1
    l = 0
    lb_logits = hg_lb_logits.astype(F32)
    assert lb_logits.shape[0] == 2
    w_in_bf = w_in[l].astype(BF16)
    w_out_bf = w_out[l].astype(BF16)
    w_mem_bf = w_mem_kv[l].astype(BF16)
    bias = sb_bias[l].astype(F32)

    w_kv_t = w_in[l][:, SB_W:3 * SB_W].T.astype(BF16)
    q_bf, rest, kt_p, vt_p = _norm_proj(x_prompt, norm_gain[l], w_in_bf, _in_proj_outs(True), tm=256,
                                        wt_bf=w_kv_t, t_widths=(SB_W, SB_W))
    mkt, mvt = _norm_proj(mem_prompt, mem_norm_gain[l], w_mem_bf, [], tm=256,
                          wt_bf=w_mem_kv[l].T.astype(BF16), t_widths=(XA_W, XA_W))
    sb_o = _sb_prompt(q_bf, kt_p, vt_p, bias)
    rest3 = rest.reshape(batch, seq, D_REST)
    hg_o, st_t = _hgrn_prompt(rest3, lb_logits)
    y_prompt = _mix_out(x_prompt, sb_o.reshape(batch, seq, SB_W), hg_o, rest3, mkt, mvt,
                        w_out_bf, hg_norm_gain[l], final_norm_gain, tm=256)
    hgrn_prompt = jnp.stack([st_t[:, h * HG_D:(h + 1) * HG_D, h * HG_D:(h + 1) * HG_D]
                             for h in range(HG_HEADS)], axis=1).swapaxes(-1, -2)

    qs_bf, k_s, v_s, rest_s = _norm_proj(x_sample.reshape(1, dec_b, d), norm_gain[l], w_in_bf,
                                         _in_proj_outs(False), tm=dec_b)
    n_pool, page = cache_k.shape[1], cache_k.shape[2]
    ckt = jnp.transpose(cache_k[l], (0, 2, 3, 1)).reshape(n_pool, SB_W, page)
    cvt = jnp.transpose(cache_v[l], (0, 2, 3, 1)).reshape(n_pool, SB_W, page)
    sb_o_s = _sb_sample(qs_bf, bias, page_table, ckt, cvt)
    hq_s = rest_s[:, SB_W:SB_W + HG_W].reshape(dec_b, HG_W, 1)
    hf_s = rest_s[:, SB_W + HG_W:SB_W + 2 * HG_W].reshape(dec_b, HG_W, 1)
    hv_s = rest_s[:, SB_W + 2 * HG_W:SB_W + 3 * HG_W].reshape(dec_b, 1, HG_W)
    hg_o_s, st_s = _hgrn_sample(state_hgrn[l].astype(F32).reshape(dec_b, HG_W, HG_D), hq_s, hf_s, hv_s, lb_logits)
    y_sample = _mix_out(x_sample, sb_o_s.reshape(dec_b, 1, SB_W), hg_o_s.reshape(dec_b, 1, HG_W),
                        rest_s.reshape(dec_b, 1, D_REST),
                        jnp.transpose(cache_mem_k[l], (0, 2, 3, 1)).reshape(dec_b, XA_W, n_mem),
                        jnp.transpose(cache_mem_v[l], (0, 2, 3, 1)).reshape(dec_b, XA_W, n_mem),
                        w_out_bf, hg_norm_gain[l], final_norm_gain, tm=1)

    def token_major(a_t, heads, dh):
        b_, _, t_ = a_t.shape
        return jnp.transpose(a_t.reshape(b_, heads, dh, t_), (0, 3, 1, 2))[None]

    return (y_prompt, y_sample, token_major(kt_p, SB_HEADS, SB_DH), token_major(vt_p, SB_HEADS, SB_DH),
            hgrn_prompt.reshape(1, batch, HG_HEADS, HG_D, HG_D),
            token_major(mkt, XA_HEADS, XA_DH), token_major(mvt, XA_HEADS, XA_DH),
            k_s.reshape(1, dec_b, 1, SB_HEADS, SB_DH), v_s.reshape(1, dec_b, 1, SB_HEADS, SB_DH),
            st_s.reshape(1, dec_b, HG_HEADS, HG_D, HG_D).astype(state_hgrn.dtype))
```

```python
import functools
import math

import numpy as np
import jax
import jax.numpy as jnp
from jax import lax
from jax.experimental import pallas as pl
from jax.experimental.pallas import tpu as pltpu

F32 = jnp.float32
BF16 = jnp.bfloat16
EPS = 1e-6

SB_HEADS, SB_DH = 8, 64
HG_HEADS, HG_D = 4, 64
XA_HEADS, XA_DH = 4, 64
SB_W = SB_HEADS * SB_DH
HG_W = HG_HEADS * HG_D
XA_W = XA_HEADS * XA_DH
D_REST = SB_W + 3 * HG_W + HG_W + 2 * XA_W

LANES = 128
SB_TQ = 512
SB_TK = 128
NEG_BIG = -1e30
LOG2E = math.log2(math.e)
HG_CHUNK = 128
HG_CHUNKS_PER_STEP = 4
PAGES_PER_STEP = 16
VMEM_LIMIT = 48 * 1024 * 1024

_NT = (((1,), (1,)), ((), ()))
_TN = (((0,), (0,)), ((), ()))


def _split_bf16(x):
    hi = x.astype(BF16)
    lo = (x - hi.astype(F32)).astype(BF16)
    return hi, lo


def _log_sigmoid_pair(t):
    sp = jnp.log1p(jnp.exp(-jnp.abs(t)))
    log_beta = jnp.minimum(t, 0.0) - sp
    return log_beta, log_beta - t


def _norm_proj_kernel(x_ref, g_ref, w_ref, *refs, outs, t_widths):
    wt_ref, out_refs = (refs[0], refs[1:]) if t_widths else (None, refs)
    x = x_ref[...]
    ms = jnp.mean(x * x, axis=-1, keepdims=True)
    xn = ((x * lax.rsqrt(ms + EPS)) * g_ref[...]).astype(BF16)
    cache = {}
    for o_ref, (lo, hi, scale) in zip(out_refs, outs):
        if (lo, hi) not in cache:
            cache[(lo, hi)] = jnp.dot(xn, w_ref[:, lo:hi], preferred_element_type=F32)
        h = cache[(lo, hi)]
        if scale != 1.0:
            h = h * scale
        o_ref[...] = h.astype(o_ref.dtype)
    off = 0
    for o_ref, width in zip(out_refs[len(outs):], t_widths):
        o_ref[0] = lax.dot_general(wt_ref[off:off + width, :], xn, _NT, preferred_element_type=F32)
        off += width


def _norm_proj(x3, gain, w_bf, outs, tm, wt_bf=None, t_widths=()):
    b, t, d = x3.shape
    m = b * t
    n = w_bf.shape[1]
    tm = min(tm, m)
    assert (t % tm == 0 or tm == m) and m % tm == 0
    nbt = max(t // tm, 1)
    assert (wt_bf is None) == (len(t_widths) == 0)
    t_args = [] if wt_bf is None else [wt_bf]
    kern = functools.partial(_norm_proj_kernel, outs=[(lo, hi, sc) for lo, hi, _, sc in outs], t_widths=tuple(t_widths))
    return pl.pallas_call(
        kern,
        grid=(m // tm,),
        in_specs=[pl.BlockSpec((tm, d), lambda i: (i, 0)),
                  pl.BlockSpec((1, d), lambda i: (0, 0)),
                  pl.BlockSpec((d, n), lambda i: (0, 0))]
                 + [pl.BlockSpec(a.shape, lambda i: (0, 0)) for a in t_args],
        out_specs=[pl.BlockSpec((tm, hi - lo), lambda i: (i, 0)) for lo, hi, _, _ in outs]
                  + [pl.BlockSpec((1, w, tm), lambda i: (i // nbt, 0, i % nbt)) for w in t_widths],
        out_shape=[jax.ShapeDtypeStruct((m, hi - lo), dt) for lo, hi, dt, _ in outs]
                  + [jax.ShapeDtypeStruct((b, w, t), F32) for w in t_widths],
        compiler_params=pltpu.CompilerParams(dimension_semantics=("arbitrary",), vmem_limit_bytes=VMEM_LIMIT),
    )(x3.reshape(m, d), gain.reshape(1, d), w_bf, *t_args)


def _suffix_matrix():
    jp = np.arange(LANES)[:, None]
    j = np.arange(LANES)[None, :]
    w = np.concatenate([(jp > j).astype(np.float32), np.ones((LANES, LANES), np.float32)], axis=1)
    return jnp.asarray(np.concatenate([w, w], axis=0), dtype=BF16)


def _sb_prompt_kernel(bias_ref, q_ref, kt_ref, vt_ref, w2_ref, o_ref, kbd_ref, vbd_ref, carry_ref, acc_ref, lb_scr, x_scr):
    tq, tk = SB_TQ, SB_TK
    hp = pl.program_id(1)
    qi = pl.program_id(2)
    nkb = kt_ref.shape[2] // tk

    @pl.when(qi == 0)
    def _():
        row_v = lax.broadcasted_iota(jnp.int32, (LANES, tk), 0)
        for j in range(nkb):
            for src, dst in ((kt_ref, kbd_ref), (vt_ref, vbd_ref)):
                blk = src[0, :, j * tk:(j + 1) * tk].astype(BF16)
                zero = jnp.zeros_like(blk)
                dst[j, :, 0:tk] = jnp.where(row_v < SB_DH, blk, zero)
                dst[j, :, tk:2 * tk] = jnp.where(row_v >= SB_DH, blk, zero)

    q2 = q_ref[...]
    b2 = (bias_ref[2 * hp] * LOG2E, bias_ref[2 * hp + 1] * LOG2E)
    qpos = qi * tq + lax.broadcasted_iota(jnp.int32, (tq, LANES), 0)
    col = lax.broadcasted_iota(jnp.int32, (tq, LANES), 1)

    carry_ref[...] = jnp.zeros_like(carry_ref)
    acc_ref[...] = jnp.zeros_like(acc_ref)
    grp = tq // tk

    def block_of(m, g):
        return (qi - m) * grp + (grp - 1 - g)

    def finish_scores(z, j, slot, g, masked):
        causal = (j * tk + col) < qpos if masked else None
        for half in range(2):
            rows = slice(half * tq, (half + 1) * tq)
            t = z[:, half * tk:(half + 1) * tk] + b2[half]
            sp = jnp.log(1.0 + jnp.exp2(-jnp.abs(t))) * LOG2E
            lb = jnp.minimum(t, 0.0) - sp
            l1m = lb - t
            if masked:
                l1m = jnp.where(causal, l1m, 0.0)
                lb = jnp.where(causal, lb, NEG_BIG)
            lb_scr[slot, g, rows, :] = lb
            x_scr[slot, g, rows, :] = l1m.astype(BF16)

    def finish_weights(r, j, slot, g):
        suffix = r[:, :LANES] + carry_ref[...]
        carry_ref[...] += r[:, LANES:]
        a = jnp.exp2(lb_scr[slot, g] + suffix).astype(BF16)
        a2 = jnp.concatenate([a[:tq], a[tq:]], axis=1)
        acc_ref[...] += lax.dot_general(a2, vbd_ref[j], _NT, preferred_element_type=F32)

    def stage(m_w, slot_w, m_s, slot_s, masked=False):
        rs, zs = [], []
        for g in range(grp):
            if m_w is not None:
                rs.append(jnp.dot(x_scr[slot_w, g], w2_ref[...], preferred_element_type=F32))
            if m_s is not None:
                zs.append(jnp.dot(q2, kbd_ref[block_of(m_s, g)], preferred_element_type=F32))
        for g in range(grp):
            if m_w is not None:
                finish_weights(rs[g], block_of(m_w, g), slot_w, g)
        for g in range(grp):
            if m_s is not None:
                finish_scores(zs[g], block_of(m_s, g), slot_s, g, masked)

    stage(None, None, 0, 0, masked=True)

    def loop_body(p, c):
        stage(2 * p, 0, 2 * p + 1, 1)
        stage(2 * p + 1, 1, 2 * p + 2, 0)
        return c

    lax.fori_loop(0, qi // 2, loop_body, 0)

    @pl.when(qi % 2 == 1)
    def _():
        stage(qi - 1, 0, qi, 1)
        stage(qi, 1, None, None)

    @pl.when(qi % 2 == 0)
    def _():
        stage(qi, 0, None, None)

    o_ref[...] = acc_ref[...]


def _sb_prompt(q_bf, kt, vt, bias):
    tq, tk = SB_TQ, SB_TK
    batch, _, seq = kt.shape
    assert seq % tq == 0 and tq % tk == 0
    nq = seq // tq
    npairs = SB_W // LANES
    return pl.pallas_call(
        _sb_prompt_kernel,
        grid=(batch, npairs, nq),
        in_specs=[pl.BlockSpec(memory_space=pltpu.SMEM),
                  pl.BlockSpec((tq, LANES), lambda b, h, i: (b * nq + i, h)),
                  pl.BlockSpec((1, LANES, seq), lambda b, h, i: (b, h, 0)),
                  pl.BlockSpec((1, LANES, seq), lambda b, h, i: (b, h, 0)),
                  pl.BlockSpec((LANES, 2 * LANES), lambda b, h, i: (0, 0))],
        out_specs=pl.BlockSpec((tq, LANES), lambda b, h, i: (b * nq + i, h)),
        out_shape=jax.ShapeDtypeStruct((batch * seq, SB_W), F32),
        scratch_shapes=[pltpu.VMEM((seq // tk, LANES, 2 * tk), BF16),
                        pltpu.VMEM((seq // tk, LANES, 2 * tk), BF16),
                        pltpu.VMEM((2 * tq, LANES), F32),
                        pltpu.VMEM((tq, LANES), F32),
                        pltpu.VMEM((2, tq // tk, 2 * tq, LANES), F32),
                        pltpu.VMEM((2, tq // tk, 2 * tq, LANES), BF16)],
        compiler_params=pltpu.CompilerParams(dimension_semantics=("arbitrary", "arbitrary", "arbitrary"),
                                             vmem_limit_bytes=VMEM_LIMIT),
    )(bias, q_bf, kt, vt, _suffix_matrix()[:LANES])


def _hgrn_levels(c):
    return [c >> (i + 1) for i in range(int(math.log2(c)))]


def _hgrn_constants(c):
    t = np.arange(c)[:, None]
    u = np.arange(c)[None, :]
    rows = [(u <= t), (u > t)]
    masks = []
    for hs in _hgrn_levels(c):
        mid = (t // (2 * hs)) * (2 * hs) + hs - 1
        upper = (t % (2 * hs)) >= hs
        rows.append((upper & (u > mid) & (u <= t)) | ((~upper) & (u > t) & (u <= mid)))
        s = u
        masks.append(((t // (2 * hs)) == (s // (2 * hs))) & upper & ((s % (2 * hs)) < hs))
    masks.append(t == u)
    mall = np.concatenate(rows, axis=0).astype(np.float32)
    mall2 = np.concatenate([mall, mall], axis=1)
    return jnp.asarray(mall2, dtype=BF16), jnp.asarray(np.stack(masks).astype(np.float32))


def _forget_lower_bound(lbl_ref):
    l0 = lbl_ref[0:1]
    l1 = lbl_ref[1:2]
    m = jnp.maximum(l0, l1)
    e0 = jnp.exp(l0 - m)
    e1 = jnp.exp(l1 - m)
    return e0 / (e0 + e1)


def _hgrn_prompt_kernel(lbl_ref, hq_ref, hf_ref, hv_ref, mall_ref, bm_ref, o_ref, st_ref, s_scr, d_scr):
    c = HG_CHUNK
    ci = pl.program_id(1)
    levels = _hgrn_levels(c)
    n_sub = hq_ref.shape[1] // c

    @pl.when(ci == 0)
    def _():
        s_scr[...] = jnp.zeros_like(s_scr)

    lb = _forget_lower_bound(lbl_ref)
    lane = lax.broadcasted_iota(jnp.int32, (c, HG_W), 1)
    head_masks = [(lane >= h * HG_D) & (lane < (h + 1) * HG_D) for h in range(HG_HEADS)]
    r = lax.broadcasted_iota(jnp.int32, (HG_W, HG_W), 0)
    cc = lax.broadcasted_iota(jnp.int32, (HG_W, HG_W), 1)
    same_head = (r // HG_D) == (cc // HG_D)

    def by_head(x_bf):
        z = jnp.zeros_like(x_bf)
        return jnp.concatenate([jnp.where(m, x_bf, z) for m in head_masks], axis=0)

    def pair_scores(q_f32, k_f32, mask):
        p = lax.dot_general(q_f32.astype(BF16), by_head(k_f32.astype(BF16)), _NT, preferred_element_type=F32)
        return p * jnp.concatenate([mask] * HG_HEADS, axis=1)

    st = s_scr[...]
    for sub in range(n_sub):
        rows = slice(sub * c, (sub + 1) * c)
        d = d_scr.at[sub]
        q = hq_ref[0, rows, :]
        f = lb + (1.0 - lb) * jax.nn.sigmoid(hf_ref[0, rows, :])
        logf = jnp.log(f)
        k = 1.0 - f
        v_bf = hv_ref[0, rows, :].astype(BF16)
        hi, lo = _split_bf16(logf)
        d[...] = jnp.dot(mall_ref[...], jnp.concatenate([hi, lo], axis=0), preferred_element_type=F32)

        attn = pair_scores(q, k, bm_ref[len(levels)])
        for li in range(len(levels)):
            e = jnp.exp(d[(2 + li) * c:(3 + li) * c, :])
            attn = attn + pair_scores(q * e, k * e, bm_ref[li])
        o = jnp.dot(attn.astype(BF16), by_head(v_bf), preferred_element_type=F32)

        bcum = d[0:c, :]
        o = o + lax.dot_general((q * jnp.exp(bcum)).astype(BF16), st.astype(BF16), _NT,
                                preferred_element_type=F32)
        o_ref[0, rows, :] = o

        k_dec = (k * jnp.exp(d[c:2 * c, :])).astype(BF16)
        upd = lax.dot_general(v_bf, k_dec, _TN, preferred_element_type=F32)
        st = st * jnp.exp(bcum[c - 1:c, :]) + jnp.where(same_head, upd, 0.0)
    s_scr[...] = st

    @pl.when(ci == pl.num_programs(1) - 1)
    def _():
        st_ref[0] = st


def _hgrn_prompt(rest3, lb_logits):
    b, t, _ = rest3.shape
    c = HG_CHUNK
    mall2, bm = _hgrn_constants(c)
    nr = mall2.shape[0]
    n_sub = HG_CHUNKS_PER_STEP if t % (HG_CHUNKS_PER_STEP * c) == 0 else 1
    rows = n_sub * c
    col = lambda j: (lambda bi, ci: (bi, ci, j))
    return pl.pallas_call(
        _hgrn_prompt_kernel,
        grid=(b, t // rows),
        in_specs=[pl.BlockSpec((2, HG_W), lambda bi, ci: (0, 0)),
                  pl.BlockSpec((1, rows, HG_W), col(2)),
                  pl.BlockSpec((1, rows, HG_W), col(3)),
                  pl.BlockSpec((1, rows, HG_W), col(4)),
                  pl.BlockSpec((nr, 2 * c), lambda bi, ci: (0, 0)),
                  pl.BlockSpec(bm.shape, lambda bi, ci: (0, 0, 0))],
        out_specs=[pl.BlockSpec((1, rows, HG_W), lambda bi, ci: (bi, ci, 0)),
                   pl.BlockSpec((1, HG_W, HG_W), lambda bi, ci: (bi, 0, 0))],
        out_shape=[jax.ShapeDtypeStruct((b, t, HG_W), F32), jax.ShapeDtypeStruct((b, HG_W, HG_W), F32)],
        scratch_shapes=[pltpu.VMEM((HG_W, HG_W), F32), pltpu.VMEM((n_sub, nr, HG_W), F32)],
        compiler_params=pltpu.CompilerParams(dimension_semantics=("arbitrary", "arbitrary"),
                                             vmem_limit_bytes=VMEM_LIMIT),
    )(lb_logits, rest3, rest3, rest3, mall2, bm)


def _silu(g):
    return g * jax.nn.sigmoid(g)


def _mix_out_kernel(x_ref, sbo_ref, hgo_ref, sbg_ref, hgg_ref, xaq_ref, xag_ref, mkt_ref, mvt_ref,
                    wout_ref, hgn_ref, bd2_ref, fng_ref, y_ref):
    tm = x_ref.shape[1]
    rows = max(tm, 8)

    def load(ref):
        a = ref[0]
        return jnp.broadcast_to(a, (rows, a.shape[-1])) if tm < rows else a

    x = load(x_ref)
    lane = lax.broadcasted_iota(jnp.int32, (rows, XA_W), 1)
    row_m = lax.broadcasted_iota(jnp.int32, (XA_W, mkt_ref.shape[2]), 0)

    xq = (load(xaq_ref) * (1.0 / math.sqrt(XA_DH))).astype(BF16)
    zq = jnp.zeros_like(xq)
    q_st = jnp.concatenate([jnp.where((lane >= h * XA_DH) & (lane < (h + 1) * XA_DH), xq, zq)
                            for h in range(XA_HEADS)], axis=0)
    s = jnp.dot(q_st, mkt_ref[0].astype(BF16), preferred_element_type=F32)
    p = jnp.exp(s - jnp.max(s, axis=-1, keepdims=True))
    p = (p / jnp.sum(p, axis=-1, keepdims=True)).astype(BF16)
    p2 = jnp.concatenate([p[h * rows:(h + 1) * rows] for h in range(XA_HEADS)], axis=1)
    mvt = mvt_ref[0].astype(BF16)
    zv = jnp.zeros_like(mvt)
    mvt_st = jnp.concatenate([jnp.where((row_m >= h * XA_DH) & (row_m < (h + 1) * XA_DH), mvt, zv)
                              for h in range(XA_HEADS)], axis=1)
    xa_o = lax.dot_general(p2, mvt_st, _NT, preferred_element_type=F32)

    hg = load(hgo_ref)
    hi, lo = _split_bf16(hg * hg)
    ms = jnp.dot(jnp.concatenate([hi, lo], axis=1), bd2_ref[...], preferred_element_type=F32) * (1.0 / HG_D)
    hg_n = (hg * lax.rsqrt(ms + EPS)) * hgn_ref[...]

    sb = (load(sbo_ref) * _silu(load(sbg_ref))).astype(BF16)
    hgm = (hg_n * _silu(load(hgg_ref))).astype(BF16)
    xam = (xa_o * _silu(load(xag_ref))).astype(BF16)
    y = x + jnp.dot(sb, wout_ref[0:SB_W, :], preferred_element_type=F32)
    y = y + jnp.dot(hgm, wout_ref[SB_W:SB_W + HG_W, :], preferred_element_type=F32)
    y = y + jnp.dot(xam, wout_ref[SB_W + HG_W:, :], preferred_element_type=F32)
    msy = jnp.mean(y * y, axis=-1, keepdims=True)
    y = (y * lax.rsqrt(msy + EPS)) * fng_ref[...]
    y_ref[0] = y[0:tm]


def _mix_out(x3, sbo3, hgo3, rest3, mkt3, mvt3, wout_bf, hg_norm_gain, final_gain, tm):
    b, t, d = x3.shape
    nm = mkt3.shape[2]
    bd = (np.arange(HG_W)[:, None] // HG_D) == (np.arange(HG_W)[None, :] // HG_D)
    bd2 = jnp.asarray(np.concatenate([bd, bd], axis=0).astype(np.float32), dtype=BF16)
    blk = lambda w, j: pl.BlockSpec((1, tm, w), lambda bi, ti: (bi, ti, j))
    const = lambda shape: pl.BlockSpec(shape, lambda bi, ti: (0,) * len(shape))
    return pl.pallas_call(
        _mix_out_kernel,
        grid=(b, t // tm),
        in_specs=[blk(d, 0), blk(SB_W, 0), blk(HG_W, 0),
                  blk(SB_W, 0),
                  blk(HG_W, 5),
                  blk(XA_W, 6),
                  blk(XA_W, 7),
                  pl.BlockSpec((1, XA_W, nm), lambda bi, ti: (bi, 0, 0)),
                  pl.BlockSpec((1, XA_W, nm), lambda bi, ti: (bi, 0, 0)),
                  const(wout_bf.shape), const((1, HG_W)), const(bd2.shape), const((1, d))],
        out_specs=blk(d, 0),
        out_shape=jax.ShapeDtypeStruct((b, t, d), F32),
        compiler_params=pltpu.CompilerParams(dimension_semantics=("arbitrary", "arbitrary"),
                                             vmem_limit_bytes=VMEM_LIMIT),
    )(x3, sbo3, hgo3, rest3, rest3, rest3, rest3, mkt3, mvt3, wout_bf,
      hg_norm_gain.reshape(1, HG_W), bd2, final_gain.reshape(1, d))


def _page_matrices():
    n = PAGES_PER_STEP * SB_HEADS
    r = np.arange(n)[:, None]
    c = np.arange(n)[None, :]
    same_head = (r % SB_HEADS) == (c % SB_HEADS)
    before = same_head & ((c // SB_HEADS) < (r // SB_HEADS))
    mats = [np.concatenate([m, m], axis=1).astype(np.float32) for m in (before, same_head)]
    return jnp.asarray(np.concatenate(mats, axis=0), dtype=BF16)


def _sb_sample_kernel(pt_ref, q_ref, bias_ref, w2_ref, pm_ref, *refs):
    npg = PAGES_PER_STEP
    kt_refs, vt_refs = refs[:npg], refs[npg:2 * npg]
    o_ref, carry_ref, acc_ref = refs[2 * npg:]
    s = pl.program_id(1)
    n = npg * SB_HEADS

    @pl.when(s == 0)
    def _():
        carry_ref[...] = jnp.zeros_like(carry_ref)
        acc_ref[...] = jnp.zeros_like(acc_ref)

    row8 = lax.broadcasted_iota(jnp.int32, (SB_HEADS, SB_W), 0)
    lane8 = lax.broadcasted_iota(jnp.int32, (SB_HEADS, SB_W), 1)
    own = (lane8 // SB_DH) == row8
    qb = jnp.broadcast_to(q_ref[0].astype(F32), (SB_HEADS, SB_W))
    q_bd = jnp.where(own, qb, 0.0).astype(BF16)

    z = jnp.concatenate([jnp.dot(q_bd, kt_refs[i][0].astype(BF16), preferred_element_type=F32)
                         for i in range(npg)], axis=0)
    bias_t = jnp.concatenate([bias_ref[...]] * npg, axis=0)
    log_beta, log_1m = _log_sigmoid_pair(z + bias_t)
    hi, lo = _split_bf16(log_1m)
    r = jnp.dot(jnp.concatenate([hi, lo], axis=1), w2_ref[...], preferred_element_type=F32)
    tot = r[:, LANES:]
    thi, tlo = _split_bf16(tot)
    cross = jnp.dot(pm_ref[...], jnp.concatenate([thi, tlo], axis=0), preferred_element_type=F32)
    carry = carry_ref[...]
    suffix = r[:, :LANES] + cross[:n] + jnp.concatenate([carry] * npg, axis=0)
    carry_ref[...] = carry + cross[n:n + SB_HEADS]
    a = jnp.exp(log_beta + suffix).astype(BF16)
    acc = acc_ref[...]
    for i in range(npg):
        acc = acc + lax.dot_general(a[i * SB_HEADS:(i + 1) * SB_HEADS], vt_refs[i][0].astype(BF16), _NT,
                                    preferred_element_type=F32)
    acc_ref[...] = acc

    @pl.when(s == pl.num_programs(1) - 1)
    def _():
        o_ref[0] = jnp.sum(jnp.where(own, acc, 0.0), axis=0, keepdims=True)


def _sb_sample(q_bf, bias, page_table, ckt, cvt):
    b = q_bf.shape[0]
    n_pages = page_table.shape[1]
    page = ckt.shape[2]
    assert page == LANES and n_pages % PAGES_PER_STEP == 0
    npg = PAGES_PER_STEP
    nsteps = n_pages // npg

    def page_spec(i):
        return pl.BlockSpec((1, SB_W, page),
                            lambda bi, si, pt: (pt[bi * n_pages + (n_pages - 1 - (si * npg + i))], 0, 0))

    n = npg * SB_HEADS
    grid_spec = pltpu.PrefetchScalarGridSpec(
        num_scalar_prefetch=1,
        grid=(b, nsteps),
        in_specs=[pl.BlockSpec((1, 1, SB_W), lambda bi, si, pt: (bi, 0, 0)),
                  pl.BlockSpec((SB_HEADS, LANES), lambda bi, si, pt: (0, 0)),
                  pl.BlockSpec((2 * LANES, 2 * LANES), lambda bi, si, pt: (0, 0)),
                  pl.BlockSpec((2 * n, 2 * n), lambda bi, si, pt: (0, 0))]
                 + [page_spec(i) for i in range(npg)] * 2,
        out_specs=pl.BlockSpec((1, 1, SB_W), lambda bi, si, pt: (bi, 0, 0)),
        scratch_shapes=[pltpu.VMEM((SB_HEADS, LANES), F32), pltpu.VMEM((SB_HEADS, SB_W), F32)])
    bias_rep = jnp.broadcast_to(bias.astype(F32)[:, None], (SB_HEADS, LANES))
    out = pl.pallas_call(
        _sb_sample_kernel,
        grid_spec=grid_spec,
        out_shape=jax.ShapeDtypeStruct((b, 1, SB_W), F32),
        compiler_params=pltpu.CompilerParams(dimension_semantics=("arbitrary", "arbitrary"),
                                             vmem_limit_bytes=VMEM_LIMIT),
    )(page_table.reshape(-1), q_bf.reshape(b, 1, SB_W), bias_rep, _suffix_matrix(), _page_matrices(),
      *([ckt] * npg), *([cvt] * npg))
    return out.reshape(b, SB_W)


def _hgrn_sample_kernel(lbl_ref, s_ref, q_ref, f_ref, v_ref, o_ref, sn_ref):
    l0 = lbl_ref[0]
    l1 = lbl_ref[1]
    m = jnp.maximum(l0, l1)
    e0 = jnp.exp(l0 - m)
    e1 = jnp.exp(l1 - m)
    lb = e0 / (e0 + e1)
    f = lb + (1.0 - lb) * jax.nn.sigmoid(f_ref[0])
    k = 1.0 - f
    v = v_ref[0]
    v_rows = jnp.concatenate([jnp.broadcast_to(v[:, h * HG_D:(h + 1) * HG_D], (HG_D, HG_D))
                              for h in range(HG_HEADS)], axis=0)
    s_new = f * s_ref[0] + k * v_rows
    sn_ref[0] = s_new
    qs = q_ref[0] * s_new
    o_ref[0] = jnp.sum(qs.reshape(HG_HEADS, HG_D, HG_D), axis=1)


def _hgrn_sample(state, q_col, f_col, v_row, lb_logits):
    b = state.shape[0]
    row = lambda shape: pl.BlockSpec((1,) + shape, lambda bi: (bi, 0, 0))
    return pl.pallas_call(
        _hgrn_sample_kernel,
        grid=(b,),
        in_specs=[pl.BlockSpec((2, HG_W, 1), lambda bi: (0, 0, 0)),
                  row((HG_W, HG_D)), row((HG_W, 1)), row((HG_W, 1)), row((1, HG_W))],
        out_specs=[row((HG_HEADS, HG_D)), row((HG_W, HG_D))],
        out_shape=[jax.ShapeDtypeStruct((b, HG_HEADS, HG_D), F32), jax.ShapeDtypeStruct((b, HG_W, HG_D), F32)],
        compiler_params=pltpu.CompilerParams(dimension_semantics=("arbitrary",)),
    )(lb_logits.reshape(2, HG_W, 1), state, q_col, f_col, v_row)


def _in_proj_outs(transposed_kv):
    qs = (LOG2E if transposed_kv else 1.0) / math.sqrt(SB_DH)
    q_out = (0, SB_W, BF16, qs)
    rest_out = (3 * SB_W, 3 * SB_W + D_REST, F32, 1.0)
    if transposed_kv:
        return [q_out, rest_out]
    return [q_out, (SB_W, 2 * SB_W, F32, 1.0), (2 * SB_W, 3 * SB_W, F32, 1.0), rest_out]


def kernel(x_prompt, x_sample, mem_prompt, cache_k, cache_v, page_table, state_hgrn, cache_mem_k, cache_mem_v,
           norm_gain, w_in, sb_bias, hg_lb_logits, hg_norm_gain, mem_norm_gain, w_mem_kv, w_out, final_norm_gain):
    batch, seq, d = x_prompt.shape
    dec_b = x_sample.shape[0]
    n_mem = mem_prompt.shape[1]
    depth = w_in.shape[0]
    assert depth == 1 and x_sample.shape[1] == 1
    l = 0
    lb_logits = hg_lb_logits.astype(F32)
    assert lb_logits.shape[0] == 2
    w_in_bf = w_in[l].astype(BF16)
    w_out_bf = w_out[l].astype(BF16)
    w_mem_bf = w_mem_kv[l].astype(BF16)
    bias = sb_bias[l].astype(F32)

    w_kv_t = w_in[l][:, SB_W:3 * SB_W].T.astype(BF16)
    q_bf, rest, kt_p, vt_p = _norm_proj(x_prompt, norm_gain[l], w_in_bf, _in_proj_outs(True), tm=256,
                                        wt_bf=w_kv_t, t_widths=(SB_W, SB_W))
    mkt, mvt = _norm_proj(mem_prompt, mem_norm_gain[l], w_mem_bf, [], tm=256,
                          wt_bf=w_mem_kv[l].T.astype(BF16), t_widths=(XA_W, XA_W))
    sb_o = _sb_prompt(q_bf, kt_p, vt_p, bias)
    rest3 = rest.reshape(batch, seq, D_REST)
    hg_o, st_t = _hgrn_prompt(rest3, lb_logits)
    y_prompt = _mix_out(x_prompt, sb_o.reshape(batch, seq, SB_W), hg_o, rest3, mkt, mvt,
                        w_out_bf, hg_norm_gain[l], final_norm_gain, tm=256)
    hgrn_prompt = jnp.stack([st_t[:, h * HG_D:(h + 1) * HG_D, h * HG_D:(h + 1) * HG_D]
                             for h in range(HG_HEADS)], axis=1).swapaxes(-1, -2)

    qs_bf, k_s, v_s, rest_s = _norm_proj(x_sample.reshape(1, dec_b, d), norm_gain[l], w_in_bf,
                                         _in_proj_outs(False), tm=dec_b)
    n_pool, page = cache_k.shape[1], cache_k.shape[2]
    ckt = jnp.transpose(cache_k[l], (0, 2, 3, 1)).reshape(n_pool, SB_W, page)
    cvt = jnp.transpose(cache_v[l], (0, 2, 3, 1)).reshape(n_pool, SB_W, page)
    sb_o_s = _sb_sample(qs_bf, bias, page_table, ckt, cvt)
    hq_s = rest_s[:, SB_W:SB_W + HG_W].reshape(dec_b, HG_W, 1)
    hf_s = rest_s[:, SB_W + HG_W:SB_W + 2 * HG_W].reshape(dec_b, HG_W, 1)
    hv_s = rest_s[:, SB_W + 2 * HG_W:SB_W + 3 * HG_W].reshape(dec_b, 1, HG_W)
    hg_o_s, st_s = _hgrn_sample(state_hgrn[l].astype(F32).reshape(dec_b, HG_W, HG_D), hq_s, hf_s, hv_s, lb_logits)
    y_sample = _mix_out(x_sample, sb_o_s.reshape(dec_b, 1, SB_W), hg_o_s.reshape(dec_b, 1, HG_W),
                        rest_s.reshape(dec_b, 1, D_REST),
                        jnp.transpose(cache_mem_k[l], (0, 2, 3, 1)).reshape(dec_b, XA_W, n_mem),
                        jnp.transpose(cache_mem_v[l], (0, 2, 3, 1)).reshape(dec_b, XA_W, n_mem),
                        w_out_bf, hg_norm_gain[l], final_norm_gain, tm=1)

    def token_major(a_t, heads, dh):
        b_, _, t_ = a_t.shape
        return jnp.transpose(a_t.reshape(b_, heads, dh, t_), (0, 3, 1, 2))[None]

    return (y_prompt, y_sample, token_major(kt_p, SB_HEADS, SB_DH), token_major(vt_p, SB_HEADS, SB_DH),
            hgrn_prompt.reshape(1, batch, HG_HEADS, HG_D, HG_D),
            token_major(mkt, XA_HEADS, XA_DH), token_major(mvt, XA_HEADS, XA_DH),
            k_s.reshape(1, dec_b, 1, SB_HEADS, SB_DH), v_s.reshape(1, dec_b, 1, SB_HEADS, SB_DH),
            st_s.reshape(1, dec_b, HG_HEADS, HG_D, HG_D).astype(state_hgrn.dtype))
```

```python
import functools
import math

import numpy as np
import jax
import jax.numpy as jnp
from jax import lax
from jax.experimental import pallas as pl
from jax.experimental.pallas import tpu as pltpu

F32 = jnp.float32
BF16 = jnp.bfloat16
EPS = 1e-6

SB_HEADS, SB_DH = 8, 64
HG_HEADS, HG_D = 4, 64
XA_HEADS, XA_DH = 4, 64
SB_W = SB_HEADS * SB_DH
HG_W = HG_HEADS * HG_D
XA_W = XA_HEADS * XA_DH
D_REST = SB_W + 3 * HG_W + HG_W + 2 * XA_W

LANES = 128
SB_TQ = 512
SB_TK = 128
NEG_BIG = -1e30
LOG2E = math.log2(math.e)
HG_CHUNK = 128
HG_CHUNKS_PER_STEP = 4
PAGES_PER_STEP = 16
VMEM_LIMIT = 48 * 1024 * 1024

_NT = (((1,), (1,)), ((), ()))
_TN = (((0,), (0,)), ((), ()))


def _split_bf16(x):
    hi = x.astype(BF16)
    lo = (x - hi.astype(F32)).astype(BF16)
    return hi, lo


def _log_sigmoid_pair(t):
    sp = jnp.log1p(jnp.exp(-jnp.abs(t)))
    log_beta = jnp.minimum(t, 0.0) - sp
    return log_beta, log_beta - t


def _norm_proj_kernel(x_ref, g_ref, w_ref, *refs, outs, t_widths):
    wt_ref, out_refs = (refs[0], refs[1:]) if t_widths else (None, refs)
    x = x_ref[...]
    ms = jnp.mean(x * x, axis=-1, keepdims=True)
    xn = ((x * lax.rsqrt(ms + EPS)) * g_ref[...]).astype(BF16)
    cache = {}
    for o_ref, (lo, hi, scale) in zip(out_refs, outs):
        if (lo, hi) not in cache:
            cache[(lo, hi)] = jnp.dot(xn, w_ref[:, lo:hi], preferred_element_type=F32)
        h = cache[(lo, hi)]
        if scale != 1.0:
            h = h * scale
        o_ref[...] = h.astype(o_ref.dtype)
    off = 0
    for o_ref, width in zip(out_refs[len(outs):], t_widths):
        o_ref[0] = lax.dot_general(wt_ref[off:off + width, :], xn, _NT, preferred_element_type=F32)
        off += width


def _norm_proj(x3, gain, w_bf, outs, tm, wt_bf=None, t_widths=()):
    b, t, d = x3.shape
    m = b * t
    n = w_bf.shape[1]
    tm = min(tm, m)
    assert (t % tm == 0 or tm == m) and m % tm == 0
    nbt = max(t // tm, 1)
    assert (wt_bf is None) == (len(t_widths) == 0)
    t_args = [] if wt_bf is None else [wt_bf]
    kern = functools.partial(_norm_proj_kernel, outs=[(lo, hi, sc) for lo, hi, _, sc in outs], t_widths=tuple(t_widths))
    return pl.pallas_call(
        kern,
        grid=(m // tm,),
        in_specs=[pl.BlockSpec((tm, d), lambda i: (i, 0)),
                  pl.BlockSpec((1, d), lambda i: (0, 0)),
                  pl.BlockSpec((d, n), lambda i: (0, 0))]
                 + [pl.BlockSpec(a.shape, lambda i: (0, 0)) for a in t_args],
        out_specs=[pl.BlockSpec((tm, hi - lo), lambda i: (i, 0)) for lo, hi, _, _ in outs]
                  + [pl.BlockSpec((1, w, tm), lambda i: (i // nbt, 0, i % nbt)) for w in t_widths],
        out_shape=[jax.ShapeDtypeStruct((m, hi - lo), dt) for lo, hi, dt, _ in outs]
                  + [jax.ShapeDtypeStruct((b, w, t), F32) for w in t_widths],
        compiler_params=pltpu.CompilerParams(dimension_semantics=("arbitrary",), vmem_limit_bytes=VMEM_LIMIT),
    )(x3.reshape(m, d), gain.reshape(1, d), w_bf, *t_args)


def _suffix_matrix():
    jp = np.arange(LANES)[:, None]
    j = np.arange(LANES)[None, :]
    w = np.concatenate([(jp > j).astype(np.float32), np.ones((LANES, LANES), np.float32)], axis=1)
    return jnp.asarray(np.concatenate([w, w], axis=0), dtype=BF16)


def _sb_prompt_kernel(bias_ref, q_ref, kt_ref, vt_ref, w2_ref, o_ref, kbd_ref, vbd_ref, carry_ref, acc_ref, lb_scr, x_scr):
    tq, tk = SB_TQ, SB_TK
    hp = pl.program_id(1)
    qi = pl.program_id(2)
    nkb = kt_ref.shape[2] // tk

    @pl.when(qi == 0)
    def _():
        row_v = lax.broadcasted_iota(jnp.int32, (LANES, tk), 0)
        for j in range(nkb):
            for src, dst in ((kt_ref, kbd_ref), (vt_ref, vbd_ref)):
                blk = src[0, :, j * tk:(j + 1) * tk].astype(BF16)
                zero = jnp.zeros_like(blk)
                dst[j, :, 0:tk] = jnp.where(row_v < SB_DH, blk, zero)
                dst[j, :, tk:2 * tk] = jnp.where(row_v >= SB_DH, blk, zero)

    q2 = q_ref[...]
    b2 = (bias_ref[2 * hp] * LOG2E, bias_ref[2 * hp + 1] * LOG2E)
    qpos = qi * tq + lax.broadcasted_iota(jnp.int32, (tq, LANES), 0)
    col = lax.broadcasted_iota(jnp.int32, (tq, LANES), 1)

    grp = tq // tk
    pend = 2

    def block_of(m, g):
        return (qi - m) * grp + (grp - 1 - g)

    def reset():
        carry_ref[...] = jnp.zeros_like(carry_ref)
        acc_ref[...] = jnp.zeros_like(acc_ref)

    def emit(tile):
        o_ref[pl.ds(pl.multiple_of(tile * tq, tq), tq), :] = acc_ref[...]

    def finish_scores(z, j, slot, g, masked):
        causal = (j * tk + col) < qpos if masked else None
        for half in range(2):
            rows = slice(half * tq, (half + 1) * tq)
            t = z[:, half * tk:(half + 1) * tk] + b2[half]
            sp = jnp.log(1.0 + jnp.exp2(-jnp.abs(t))) * LOG2E
            lb = jnp.minimum(t, 0.0) - sp
            l1m = lb - t
            if masked:
                l1m = jnp.where(causal, l1m, 0.0)
                lb = jnp.where(causal, lb, NEG_BIG)
            lb_scr[slot, g, rows, :] = lb
            x_scr[slot, g, rows, :] = l1m.astype(BF16)

    def finish_weights(r, j, slot, g):
        suffix = r[:, :LANES] + carry_ref[...]
        carry_ref[...] += r[:, LANES:]
        a = jnp.exp2(lb_scr[slot, g] + suffix).astype(BF16)
        a2 = jnp.concatenate([a[:tq], a[tq:]], axis=1)
        acc_ref[...] += lax.dot_general(a2, vbd_ref[j], _NT, preferred_element_type=F32)

    def stage(w_blocks, slot_w, m_s, slot_s, masked=False, between=None):
        rs, zs = [], []
        for g in range(grp):
            if w_blocks is not None:
                rs.append(jnp.dot(x_scr[slot_w, g], w2_ref[...], preferred_element_type=F32))
            if m_s is not None:
                zs.append(jnp.dot(q2, kbd_ref[block_of(m_s, g)], preferred_element_type=F32))
        for g in range(grp):
            if w_blocks is not None:
                finish_weights(rs[g], w_blocks[g], slot_w, g)
        if between is not None:
            between()
        for g in range(grp):
            if m_s is not None:
                finish_scores(zs[g], block_of(m_s, g), slot_s, g, masked)

    def blocks(m):
        return [block_of(m, g) for g in range(grp)]

    last_blocks = [grp - 1 - g for g in range(grp)]

    @pl.when(qi == 0)
    def _():
        reset()
        stage(None, None, 0, pend, masked=True)

    @pl.when(qi > 0)
    def _():
        def between():
            emit(qi - 1)
            reset()
        stage(last_blocks, pend, 0, 0, masked=True, between=between)

    def loop_body(p, c):
        stage(blocks(2 * p), 0, 2 * p + 1, 1)
        stage(blocks(2 * p + 1), 1, 2 * p + 2, 0)
        return c

    lax.fori_loop(0, (qi - 1) // 2, loop_body, 0)

    @pl.when(qi % 2 == 1)
    def _():
        stage(blocks(qi - 1), 0, qi, pend)

    @pl.when((qi % 2 == 0) & (qi > 0))
    def _():
        stage(blocks(qi - 2), 0, qi - 1, 1)
        stage(blocks(qi - 1), 1, qi, pend)

    @pl.when(qi == pl.num_programs(2) - 1)
    def _():
        stage(last_blocks, pend, None, None)
        emit(qi)


def _sb_prompt(q_bf, kt, vt, bias):
    tq, tk = SB_TQ, SB_TK
    batch, _, seq = kt.shape
    assert seq % tq == 0 and tq % tk == 0
    nq = seq // tq
    npairs = SB_W // LANES
    return pl.pallas_call(
        _sb_prompt_kernel,
        grid=(batch, npairs, nq),
        in_specs=[pl.BlockSpec(memory_space=pltpu.SMEM),
                  pl.BlockSpec((tq, LANES), lambda b, h, i: (b * nq + i, h)),
                  pl.BlockSpec((1, LANES, seq), lambda b, h, i: (b, h, 0)),
                  pl.BlockSpec((1, LANES, seq), lambda b, h, i: (b, h, 0)),
                  pl.BlockSpec((LANES, 2 * LANES), lambda b, h, i: (0, 0))],
        out_specs=pl.BlockSpec((seq, LANES), lambda b, h, i: (b, h)),
        out_shape=jax.ShapeDtypeStruct((batch * seq, SB_W), F32),
        scratch_shapes=[pltpu.VMEM((seq // tk, LANES, 2 * tk), BF16),
                        pltpu.VMEM((seq // tk, LANES, 2 * tk), BF16),
                        pltpu.VMEM((2 * tq, LANES), F32),
                        pltpu.VMEM((tq, LANES), F32),
                        pltpu.VMEM((3, tq // tk, 2 * tq, LANES), F32),
                        pltpu.VMEM((3, tq // tk, 2 * tq, LANES), BF16)],
        compiler_params=pltpu.CompilerParams(dimension_semantics=("arbitrary", "arbitrary", "arbitrary"),
                                             vmem_limit_bytes=VMEM_LIMIT),
    )(bias, q_bf, kt, vt, _suffix_matrix()[:LANES])


def _hgrn_levels(c):
    return [c >> (i + 1) for i in range(int(math.log2(c)))]


def _hgrn_constants(c):
    t = np.arange(c)[:, None]
    u = np.arange(c)[None, :]
    rows = [(u <= t), (u > t)]
    masks = []
    for hs in _hgrn_levels(c):
        mid = (t // (2 * hs)) * (2 * hs) + hs - 1
        upper = (t % (2 * hs)) >= hs
        rows.append((upper & (u > mid) & (u <= t)) | ((~upper) & (u > t) & (u <= mid)))
        s = u
        masks.append(((t // (2 * hs)) == (s // (2 * hs))) & upper & ((s % (2 * hs)) < hs))
    masks.append(t == u)
    mall = np.concatenate(rows, axis=0).astype(np.float32)
    mall2 = np.concatenate([mall, mall], axis=1)
    return jnp.asarray(mall2, dtype=BF16), jnp.asarray(np.stack(masks).astype(np.float32))


def _forget_lower_bound(lbl_ref):
    l0 = lbl_ref[0:1]
    l1 = lbl_ref[1:2]
    m = jnp.maximum(l0, l1)
    e0 = jnp.exp(l0 - m)
    e1 = jnp.exp(l1 - m)
    return e0 / (e0 + e1)


def _hgrn_prompt_kernel(lbl_ref, hq_ref, hf_ref, hv_ref, mall_ref, bm_ref, o_ref, st_ref, s_scr, d_scr):
    c = HG_CHUNK
    ci = pl.program_id(1)
    levels = _hgrn_levels(c)
    n_sub = hq_ref.shape[1] // c

    @pl.when(ci == 0)
    def _():
        s_scr[...] = jnp.zeros_like(s_scr)

    lb = _forget_lower_bound(lbl_ref)
    lane = lax.broadcasted_iota(jnp.int32, (c, HG_W), 1)
    head_masks = [(lane >= h * HG_D) & (lane < (h + 1) * HG_D) for h in range(HG_HEADS)]
    r = lax.broadcasted_iota(jnp.int32, (HG_W, HG_W), 0)
    cc = lax.broadcasted_iota(jnp.int32, (HG_W, HG_W), 1)
    same_head = (r // HG_D) == (cc // HG_D)

    def by_head(x_bf):
        z = jnp.zeros_like(x_bf)
        return jnp.concatenate([jnp.where(m, x_bf, z) for m in head_masks], axis=0)

    def pair_scores(q_f32, k_f32, mask):
        p = lax.dot_general(q_f32.astype(BF16), by_head(k_f32.astype(BF16)), _NT, preferred_element_type=F32)
        return p * jnp.concatenate([mask] * HG_HEADS, axis=1)

    st = s_scr[...]
    for sub in range(n_sub):
        rows = slice(sub * c, (sub + 1) * c)
        d = d_scr.at[sub]
        q = hq_ref[0, rows, :]
        f = lb + (1.0 - lb) * jax.nn.sigmoid(hf_ref[0, rows, :])
        logf = jnp.log(f)
        k = 1.0 - f
        v_bf = hv_ref[0, rows, :].astype(BF16)
        hi, lo = _split_bf16(logf)
        d[...] = jnp.dot(mall_ref[...], jnp.concatenate([hi, lo], axis=0), preferred_element_type=F32)

        attn = pair_scores(q, k, bm_ref[len(levels)])
        for li in range(len(levels)):
            e = jnp.exp(d[(2 + li) * c:(3 + li) * c, :])
            attn = attn + pair_scores(q * e, k * e, bm_ref[li])
        o = jnp.dot(attn.astype(BF16), by_head(v_bf), preferred_element_type=F32)

        bcum = d[0:c, :]
        o = o + lax.dot_general((q * jnp.exp(bcum)).astype(BF16), st.astype(BF16), _NT,
                                preferred_element_type=F32)
        o_ref[0, rows, :] = o

        k_dec = (k * jnp.exp(d[c:2 * c, :])).astype(BF16)
        upd = lax.dot_general(v_bf, k_dec, _TN, preferred_element_type=F32)
        st = st * jnp.exp(bcum[c - 1:c, :]) + jnp.where(same_head, upd, 0.0)
    s_scr[...] = st

    @pl.when(ci == pl.num_programs(1) - 1)
    def _():
        st_ref[0] = st


def _hgrn_prompt(rest3, lb_logits):
    b, t, _ = rest3.shape
    c = HG_CHUNK
    mall2, bm = _hgrn_constants(c)
    nr = mall2.shape[0]
    n_sub = HG_CHUNKS_PER_STEP if t % (HG_CHUNKS_PER_STEP * c) == 0 else 1
    rows = n_sub * c
    col = lambda j: (lambda bi, ci: (bi, ci, j))
    return pl.pallas_call(
        _hgrn_prompt_kernel,
        grid=(b, t // rows),
        in_specs=[pl.BlockSpec((2, HG_W), lambda bi, ci: (0, 0)),
                  pl.BlockSpec((1, rows, HG_W), col(2)),
                  pl.BlockSpec((1, rows, HG_W), col(3)),
                  pl.BlockSpec((1, rows, HG_W), col(4)),
                  pl.BlockSpec((nr, 2 * c), lambda bi, ci: (0, 0)),
                  pl.BlockSpec(bm.shape, lambda bi, ci: (0, 0, 0))],
        out_specs=[pl.BlockSpec((1, rows, HG_W), lambda bi, ci: (bi, ci, 0)),
                   pl.BlockSpec((1, HG_W, HG_W), lambda bi, ci: (bi, 0, 0))],
        out_shape=[jax.ShapeDtypeStruct((b, t, HG_W), F32), jax.ShapeDtypeStruct((b, HG_W, HG_W), F32)],
        scratch_shapes=[pltpu.VMEM((HG_W, HG_W), F32), pltpu.VMEM((n_sub, nr, HG_W), F32)],
        compiler_params=pltpu.CompilerParams(dimension_semantics=("arbitrary", "arbitrary"),
                                             vmem_limit_bytes=VMEM_LIMIT),
    )(lb_logits, rest3, rest3, rest3, mall2, bm)


def _silu(g):
    return g * jax.nn.sigmoid(g)


def _mix_out_kernel(x_ref, sbo_ref, hgo_ref, sbg_ref, hgg_ref, xaq_ref, xag_ref, mkt_ref, mvt_ref,
                    wout_ref, hgn_ref, bd2_ref, fng_ref, y_ref):
    tm = x_ref.shape[1]
    rows = max(tm, 8)

    def load(ref):
        a = ref[0]
        return jnp.broadcast_to(a, (rows, a.shape[-1])) if tm < rows else a

    x = load(x_ref)
    lane = lax.broadcasted_iota(jnp.int32, (rows, XA_W), 1)
    row_m = lax.broadcasted_iota(jnp.int32, (XA_W, mkt_ref.shape[2]), 0)

    xq = (load(xaq_ref) * (1.0 / math.sqrt(XA_DH))).astype(BF16)
    zq = jnp.zeros_like(xq)
    q_st = jnp.concatenate([jnp.where((lane >= h * XA_DH) & (lane < (h + 1) * XA_DH), xq, zq)
                            for h in range(XA_HEADS)], axis=0)
    s = jnp.dot(q_st, mkt_ref[0].astype(BF16), preferred_element_type=F32)
    p = jnp.exp(s - jnp.max(s, axis=-1, keepdims=True))
    p = (p / jnp.sum(p, axis=-1, keepdims=True)).astype(BF16)
    p2 = jnp.concatenate([p[h * rows:(h + 1) * rows] for h in range(XA_HEADS)], axis=1)
    mvt = mvt_ref[0].astype(BF16)
    zv = jnp.zeros_like(mvt)
    mvt_st = jnp.concatenate([jnp.where((row_m >= h * XA_DH) & (row_m < (h + 1) * XA_DH), mvt, zv)
                              for h in range(XA_HEADS)], axis=1)
    xa_o = lax.dot_general(p2, mvt_st, _NT, preferred_element_type=F32)

    hg = load(hgo_ref)
    hi, lo = _split_bf16(hg * hg)
    ms = jnp.dot(jnp.concatenate([hi, lo], axis=1), bd2_ref[...], preferred_element_type=F32) * (1.0 / HG_D)
    hg_n = (hg * lax.rsqrt(ms + EPS)) * hgn_ref[...]

    sb = (load(sbo_ref) * _silu(load(sbg_ref))).astype(BF16)
    hgm = (hg_n * _silu(load(hgg_ref))).astype(BF16)
    xam = (xa_o * _silu(load(xag_ref))).astype(BF16)
    y = x + jnp.dot(sb, wout_ref[0:SB_W, :], preferred_element_type=F32)
    y = y + jnp.dot(hgm, wout_ref[SB_W:SB_W + HG_W, :], preferred_element_type=F32)
    y = y + jnp.dot(xam, wout_ref[SB_W + HG_W:, :], preferred_element_type=F32)
    msy = jnp.mean(y * y, axis=-1, keepdims=True)
    y = (y * lax.rsqrt(msy + EPS)) * fng_ref[...]
    y_ref[0] = y[0:tm]


def _mix_out(x3, sbo3, hgo3, rest3, mkt3, mvt3, wout_bf, hg_norm_gain, final_gain, tm):
    b, t, d = x3.shape
    nm = mkt3.shape[2]
    bd = (np.arange(HG_W)[:, None] // HG_D) == (np.arange(HG_W)[None, :] // HG_D)
    bd2 = jnp.asarray(np.concatenate([bd, bd], axis=0).astype(np.float32), dtype=BF16)
    blk = lambda w, j: pl.BlockSpec((1, tm, w), lambda bi, ti: (bi, ti, j))
    const = lambda shape: pl.BlockSpec(shape, lambda bi, ti: (0,) * len(shape))
    return pl.pallas_call(
        _mix_out_kernel,
        grid=(b, t // tm),
        in_specs=[blk(d, 0), blk(SB_W, 0), blk(HG_W, 0),
                  blk(SB_W, 0),
                  blk(HG_W, 5),
                  blk(XA_W, 6),
                  blk(XA_W, 7),
                  pl.BlockSpec((1, XA_W, nm), lambda bi, ti: (bi, 0, 0)),
                  pl.BlockSpec((1, XA_W, nm), lambda bi, ti: (bi, 0, 0)),
                  const(wout_bf.shape), const((1, HG_W)), const(bd2.shape), const((1, d))],
        out_specs=blk(d, 0),
        out_shape=jax.ShapeDtypeStruct((b, t, d), F32),
        compiler_params=pltpu.CompilerParams(dimension_semantics=("arbitrary", "arbitrary"),
                                             vmem_limit_bytes=VMEM_LIMIT),
    )(x3, sbo3, hgo3, rest3, rest3, rest3, rest3, mkt3, mvt3, wout_bf,
      hg_norm_gain.reshape(1, HG_W), bd2, final_gain.reshape(1, d))


def _page_matrices():
    n = PAGES_PER_STEP * SB_HEADS
    r = np.arange(n)[:, None]
    c = np.arange(n)[None, :]
    same_head = (r % SB_HEADS) == (c % SB_HEADS)
    before = same_head & ((c // SB_HEADS) < (r // SB_HEADS))
    mats = [np.concatenate([m, m], axis=1).astype(np.float32) for m in (before, same_head)]
    return jnp.asarray(np.concatenate(mats, axis=0), dtype=BF16)


def _sb_sample_kernel(pt_ref, q_ref, bias_ref, w2_ref, pm_ref, *refs):
    npg = PAGES_PER_STEP
    kt_refs, vt_refs = refs[:npg], refs[npg:2 * npg]
    o_ref, carry_ref, acc_ref = refs[2 * npg:]
    s = pl.program_id(1)
    n = npg * SB_HEADS

    @pl.when(s == 0)
    def _():
        carry_ref[...] = jnp.zeros_like(carry_ref)
        acc_ref[...] = jnp.zeros_like(acc_ref)

    row8 = lax.broadcasted_iota(jnp.int32, (SB_HEADS, SB_W), 0)
    lane8 = lax.broadcasted_iota(jnp.int32, (SB_HEADS, SB_W), 1)
    own = (lane8 // SB_DH) == row8
    qb = jnp.broadcast_to(q_ref[0].astype(F32), (SB_HEADS, SB_W))
    q_bd = jnp.where(own, qb, 0.0).astype(BF16)

    z = jnp.concatenate([jnp.dot(q_bd, kt_refs[i][0].astype(BF16), preferred_element_type=F32)
                         for i in range(npg)], axis=0)
    bias_t = jnp.concatenate([bias_ref[...]] * npg, axis=0)
    log_beta, log_1m = _log_sigmoid_pair(z + bias_t)
    hi, lo = _split_bf16(log_1m)
    r = jnp.dot(jnp.concatenate([hi, lo], axis=1), w2_ref[...], preferred_element_type=F32)
    tot = r[:, LANES:]
    thi, tlo = _split_bf16(tot)
    cross = jnp.dot(pm_ref[...], jnp.concatenate([thi, tlo], axis=0), preferred_element_type=F32)
    carry = carry_ref[...]
    suffix = r[:, :LANES] + cross[:n] + jnp.concatenate([carry] * npg, axis=0)
    carry_ref[...] = carry + cross[n:n + SB_HEADS]
    a = jnp.exp(log_beta + suffix).astype(BF16)
    acc = acc_ref[...]
    for i in range(npg):
        acc = acc + lax.dot_general(a[i * SB_HEADS:(i + 1) * SB_HEADS], vt_refs[i][0].astype(BF16), _NT,
                                    preferred_element_type=F32)
    acc_ref[...] = acc

    @pl.when(s == pl.num_programs(1) - 1)
    def _():
        o_ref[0] = jnp.sum(jnp.where(own, acc, 0.0), axis=0, keepdims=True)


def _sb_sample(q_bf, bias, page_table, ckt, cvt):
    b = q_bf.shape[0]
    n_pages = page_table.shape[1]
    page = ckt.shape[2]
    assert page == LANES and n_pages % PAGES_PER_STEP == 0
    npg = PAGES_PER_STEP
    nsteps = n_pages // npg

    def page_spec(i):
        return pl.BlockSpec((1, SB_W, page),
                            lambda bi, si, pt: (pt[bi * n_pages + (n_pages - 1 - (si * npg + i))], 0, 0))

    n = npg * SB_HEADS
    grid_spec = pltpu.PrefetchScalarGridSpec(
        num_scalar_prefetch=1,
        grid=(b, nsteps),
        in_specs=[pl.BlockSpec((1, 1, SB_W), lambda bi, si, pt: (bi, 0, 0)),
                  pl.BlockSpec((SB_HEADS, LANES), lambda bi, si, pt: (0, 0)),
                  pl.BlockSpec((2 * LANES, 2 * LANES), lambda bi, si, pt: (0, 0)),
                  pl.BlockSpec((2 * n, 2 * n), lambda bi, si, pt: (0, 0))]
                 + [page_spec(i) for i in range(npg)] * 2,
        out_specs=pl.BlockSpec((1, 1, SB_W), lambda bi, si, pt: (bi, 0, 0)),
        scratch_shapes=[pltpu.VMEM((SB_HEADS, LANES), F32), pltpu.VMEM((SB_HEADS, SB_W), F32)])
    bias_rep = jnp.broadcast_to(bias.astype(F32)[:, None], (SB_HEADS, LANES))
    out = pl.pallas_call(
        _sb_sample_kernel,
        grid_spec=grid_spec,
        out_shape=jax.ShapeDtypeStruct((b, 1, SB_W), F32),
        compiler_params=pltpu.CompilerParams(dimension_semantics=("arbitrary", "arbitrary"),
                                             vmem_limit_bytes=VMEM_LIMIT),
    )(page_table.reshape(-1), q_bf.reshape(b, 1, SB_W), bias_rep, _suffix_matrix(), _page_matrices(),
      *([ckt] * npg), *([cvt] * npg))
    return out.reshape(b, SB_W)


def _hgrn_sample_kernel(lbl_ref, s_ref, q_ref, f_ref, v_ref, o_ref, sn_ref):
    l0 = lbl_ref[0]
    l1 = lbl_ref[1]
    m = jnp.maximum(l0, l1)
    e0 = jnp.exp(l0 - m)
    e1 = jnp.exp(l1 - m)
    lb = e0 / (e0 + e1)
    f = lb + (1.0 - lb) * jax.nn.sigmoid(f_ref[0])
    k = 1.0 - f
    v = v_ref[0]
    v_rows = jnp.concatenate([jnp.broadcast_to(v[:, h * HG_D:(h + 1) * HG_D], (HG_D, HG_D))
                              for h in range(HG_HEADS)], axis=0)
    s_new = f * s_ref[0] + k * v_rows
    sn_ref[0] = s_new
    qs = q_ref[0] * s_new
    o_ref[0] = jnp.sum(qs.reshape(HG_HEADS, HG_D, HG_D), axis=1)


def _hgrn_sample(state, q_col, f_col, v_row, lb_logits):
    b = state.shape[0]
    row = lambda shape: pl.BlockSpec((1,) + shape, lambda bi: (bi, 0, 0))
    return pl.pallas_call(
        _hgrn_sample_kernel,
        grid=(b,),
        in_specs=[pl.BlockSpec((2, HG_W, 1), lambda bi: (0, 0, 0)),
                  row((HG_W, HG_D)), row((HG_W, 1)), row((HG_W, 1)), row((1, HG_W))],
        out_specs=[row((HG_HEADS, HG_D)), row((HG_W, HG_D))],
        out_shape=[jax.ShapeDtypeStruct((b, HG_HEADS, HG_D), F32), jax.ShapeDtypeStruct((b, HG_W, HG_D), F32)],
        compiler_params=pltpu.CompilerParams(dimension_semantics=("arbitrary",)),
    )(lb_logits.reshape(2, HG_W, 1), state, q_col, f_col, v_row)


def _in_proj_outs(transposed_kv):
    qs = (LOG2E if transposed_kv else 1.0) / math.sqrt(SB_DH)
    q_out = (0, SB_W, BF16, qs)
    rest_out = (3 * SB_W, 3 * SB_W + D_REST, F32, 1.0)
    if transposed_kv:
        return [q_out, rest_out]
    return [q_out, (SB_W, 2 * SB_W, F32, 1.0), (2 * SB_W, 3 * SB_W, F32, 1.0), rest_out]


def kernel(x_prompt, x_sample, mem_prompt, cache_k, cache_v, page_table, state_hgrn, cache_mem_k, cache_mem_v,
           norm_gain, w_in, sb_bias, hg_lb_logits, hg_norm_gain, mem_norm_gain, w_mem_kv, w_out, final_norm_gain):
    batch, seq, d = x_prompt.shape
    dec_b = x_sample.shape[0]
    n_mem = mem_prompt.shape[1]
    depth = w_in.shape[0]
    assert depth == 1 and x_sample.shape[1] == 1
    l = 0
    lb_logits = hg_lb_logits.astype(F32)
    assert lb_logits.shape[0] == 2
    w_in_bf = w_in[l].astype(BF16)
    w_out_bf = w_out[l].astype(BF16)
    w_mem_bf = w_mem_kv[l].astype(BF16)
    bias = sb_bias[l].astype(F32)

    w_kv_t = w_in[l][:, SB_W:3 * SB_W].T.astype(BF16)
    q_bf, rest, kt_p, vt_p = _norm_proj(x_prompt, norm_gain[l], w_in_bf, _in_proj_outs(True), tm=256,
                                        wt_bf=w_kv_t, t_widths=(SB_W, SB_W))
    mkt, mvt = _norm_proj(mem_prompt, mem_norm_gain[l], w_mem_bf, [], tm=256,
                          wt_bf=w_mem_kv[l].T.astype(BF16), t_widths=(XA_W, XA_W))
    sb_o = _sb_prompt(q_bf, kt_p, vt_p, bias)
    rest3 = rest.reshape(batch, seq, D_REST)
    hg_o, st_t = _hgrn_prompt(rest3, lb_logits)
    y_prompt = _mix_out(x_prompt, sb_o.reshape(batch, seq, SB_W), hg_o, rest3, mkt, mvt,
                        w_out_bf, hg_norm_gain[l], final_norm_gain, tm=512)
    hgrn_prompt = jnp.stack([st_t[:, h * HG_D:(h + 1) * HG_D, h * HG_D:(h + 1) * HG_D]
                             for h in range(HG_HEADS)], axis=1).swapaxes(-1, -2)

    qs_bf, k_s, v_s, rest_s = _norm_proj(x_sample.reshape(1, dec_b, d), norm_gain[l], w_in_bf,
                                         _in_proj_outs(False), tm=dec_b)
    n_pool, page = cache_k.shape[1], cache_k.shape[2]
    ckt = jnp.transpose(cache_k[l], (0, 2, 3, 1)).reshape(n_pool, SB_W, page)
    cvt = jnp.transpose(cache_v[l], (0, 2, 3, 1)).reshape(n_pool, SB_W, page)
    sb_o_s = _sb_sample(qs_bf, bias, page_table, ckt, cvt)
    hq_s = rest_s[:, SB_W:SB_W + HG_W].reshape(dec_b, HG_W, 1)
    hf_s = rest_s[:, SB_W + HG_W:SB_W + 2 * HG_W].reshape(dec_b, HG_W, 1)
    hv_s = rest_s[:, SB_W + 2 * HG_W:SB_W + 3 * HG_W].reshape(dec_b, 1, HG_W)
    hg_o_s, st_s = _hgrn_sample(state_hgrn[l].astype(F32).reshape(dec_b, HG_W, HG_D), hq_s, hf_s, hv_s, lb_logits)
    y_sample = _mix_out(x_sample, sb_o_s.reshape(dec_b, 1, SB_W), hg_o_s.reshape(dec_b, 1, HG_W),
                        rest_s.reshape(dec_b, 1, D_REST),
                        jnp.transpose(cache_mem_k[l], (0, 2, 3, 1)).reshape(dec_b, XA_W, n_mem),
                        jnp.transpose(cache_mem_v[l], (0, 2, 3, 1)).reshape(dec_b, XA_W, n_mem),
                        w_out_bf, hg_norm_gain[l], final_norm_gain, tm=1)

    def token_major(a_t, heads, dh):
        b_, _, t_ = a_t.shape
        return jnp.transpose(a_t.reshape(b_, heads, dh, t_), (0, 3, 1, 2))[None]

    return (y_prompt, y_sample, token_major(kt_p, SB_HEADS, SB_DH), token_major(vt_p, SB_HEADS, SB_DH),
            hgrn_prompt.reshape(1, batch, HG_HEADS, HG_D, HG_D),
            token_major(mkt, XA_HEADS, XA_DH), token_major(mvt, XA_HEADS, XA_DH),
            k_s.reshape(1, dec_b, 1, SB_HEADS, SB_DH), v_s.reshape(1, dec_b, 1, SB_HEADS, SB_DH),
            st_s.reshape(1, dec_b, HG_HEADS, HG_D, HG_D).astype(state_hgrn.dtype))
```

```python
import functools
import math

import numpy as np
import jax
import jax.numpy as jnp
from jax import lax
from jax.experimental import pallas as pl
from jax.experimental.pallas import tpu as pltpu

F32 = jnp.float32
BF16 = jnp.bfloat16
EPS = 1e-6

SB_HEADS, SB_DH = 8, 64
HG_HEADS, HG_D = 4, 64
XA_HEADS, XA_DH = 4, 64
SB_W = SB_HEADS * SB_DH
HG_W = HG_HEADS * HG_D
XA_W = XA_HEADS * XA_DH
D_REST = SB_W + 3 * HG_W + HG_W + 2 * XA_W

LANES = 128
SB_TQ = 512
SB_TK = 128
NEG_BIG = -1e30
LOG2E = math.log2(math.e)
HG_CHUNK = 128
HG_CHUNKS_PER_STEP = 4
PAGES_PER_STEP = 16
VMEM_LIMIT = 48 * 1024 * 1024
SB_VMEM_LIMIT = 56 * 1024 * 1024

_NT = (((1,), (1,)), ((), ()))
_TN = (((0,), (0,)), ((), ()))


def _split_bf16(x):
    hi = x.astype(BF16)
    lo = (x - hi.astype(F32)).astype(BF16)
    return hi, lo


def _log_sigmoid_pair(t):
    sp = jnp.log1p(jnp.exp(-jnp.abs(t)))
    log_beta = jnp.minimum(t, 0.0) - sp
    return log_beta, log_beta - t


def _norm_proj_kernel(x_ref, g_ref, w_ref, *refs, outs, t_widths):
    wt_ref, out_refs = (refs[0], refs[1:]) if t_widths else (None, refs)
    x = x_ref[...]
    ms = jnp.mean(x * x, axis=-1, keepdims=True)
    xn = ((x * lax.rsqrt(ms + EPS)) * g_ref[...]).astype(BF16)
    cache = {}
    for o_ref, (lo, hi, scale) in zip(out_refs, outs):
        if (lo, hi) not in cache:
            cache[(lo, hi)] = jnp.dot(xn, w_ref[:, lo:hi], preferred_element_type=F32)
        h = cache[(lo, hi)]
        if scale != 1.0:
            h = h * scale
        o_ref[...] = h.astype(o_ref.dtype)
    off = 0
    for o_ref, width in zip(out_refs[len(outs):], t_widths):
        o_ref[0] = lax.dot_general(wt_ref[off:off + width, :], xn, _NT, preferred_element_type=F32)
        off += width


def _norm_proj(x3, gain, w_bf, outs, tm, wt_bf=None, t_widths=()):
    b, t, d = x3.shape
    m = b * t
    n = w_bf.shape[1]
    tm = min(tm, m)
    assert (t % tm == 0 or tm == m) and m % tm == 0
    nbt = max(t // tm, 1)
    assert (wt_bf is None) == (len(t_widths) == 0)
    t_args = [] if wt_bf is None else [wt_bf]
    kern = functools.partial(_norm_proj_kernel, outs=[(lo, hi, sc) for lo, hi, _, sc in outs], t_widths=tuple(t_widths))
    return pl.pallas_call(
        kern,
        grid=(m // tm,),
        in_specs=[pl.BlockSpec((tm, d), lambda i: (i, 0)),
                  pl.BlockSpec((1, d), lambda i: (0, 0)),
                  pl.BlockSpec((d, n), lambda i: (0, 0))]
                 + [pl.BlockSpec(a.shape, lambda i: (0, 0)) for a in t_args],
        out_specs=[pl.BlockSpec((tm, hi - lo), lambda i: (i, 0)) for lo, hi, _, _ in outs]
                  + [pl.BlockSpec((1, w, tm), lambda i: (i // nbt, 0, i % nbt)) for w in t_widths],
        out_shape=[jax.ShapeDtypeStruct((m, hi - lo), dt) for lo, hi, dt, _ in outs]
                  + [jax.ShapeDtypeStruct((b, w, t), F32) for w in t_widths],
        compiler_params=pltpu.CompilerParams(dimension_semantics=("arbitrary",), vmem_limit_bytes=VMEM_LIMIT),
    )(x3.reshape(m, d), gain.reshape(1, d), w_bf, *t_args)


def _suffix_matrix():
    jp = np.arange(LANES)[:, None]
    j = np.arange(LANES)[None, :]
    w = np.concatenate([(jp > j).astype(np.float32), np.ones((LANES, LANES), np.float32)], axis=1)
    return jnp.asarray(np.concatenate([w, w], axis=0), dtype=BF16)


def _sb_prompt_kernel(pt_ref, bias_ref, q_ref, kt_ref, vt_ref, w2_ref, *refs, npg, sample_steps):
    tq, tk = SB_TQ, SB_TK
    hp = pl.program_id(1)
    qi = pl.program_id(2)
    nkb = kt_ref.shape[2] // tk
    if npg:
        qs_ref, bias_s_ref, w2s_ref, pm_ref = refs[:4]
        page_refs = refs[4:4 + 2 * npg]
        o_ref, os_ref, kbd_ref, vbd_ref, carry_ref, acc_ref, lb_scr, x_scr, carry_s_ref, acc_s_ref = refs[4 + 2 * npg:]
        lin = (pl.program_id(0) * pl.num_programs(1) + hp) * pl.num_programs(2) + qi

        @pl.when(lin == 0)
        def _():
            carry_s_ref[...] = jnp.zeros_like(carry_s_ref)
            acc_s_ref[...] = jnp.zeros_like(acc_s_ref)

        def sample():
            _sb_sample_step(lin % sample_steps == 0, qs_ref, bias_s_ref, w2s_ref, pm_ref,
                            page_refs[:npg], page_refs[npg:], os_ref, carry_s_ref, acc_s_ref)
    else:
        o_ref, kbd_ref, vbd_ref, carry_ref, acc_ref, lb_scr, x_scr = refs
        sample = None

    @pl.when(qi == 0)
    def _():
        row_v = lax.broadcasted_iota(jnp.int32, (LANES, tk), 0)
        for j in range(nkb):
            for src, dst in ((kt_ref, kbd_ref), (vt_ref, vbd_ref)):
                blk = src[0, :, j * tk:(j + 1) * tk].astype(BF16)
                zero = jnp.zeros_like(blk)
                dst[j, :, 0:tk] = jnp.where(row_v < SB_DH, blk, zero)
                dst[j, :, tk:2 * tk] = jnp.where(row_v >= SB_DH, blk, zero)

    q2 = q_ref[...]
    b2 = (bias_ref[2 * hp] * LOG2E, bias_ref[2 * hp + 1] * LOG2E)
    qpos = qi * tq + lax.broadcasted_iota(jnp.int32, (tq, LANES), 0)
    col = lax.broadcasted_iota(jnp.int32, (tq, LANES), 1)

    grp = tq // tk
    pend = 2

    def block_of(m, g):
        return (qi - m) * grp + (grp - 1 - g)

    def reset():
        carry_ref[...] = jnp.zeros_like(carry_ref)
        acc_ref[...] = jnp.zeros_like(acc_ref)

    def emit(tile):
        o_ref[pl.ds(pl.multiple_of(tile * tq, tq), tq), :] = acc_ref[...]

    def finish_scores(z, j, slot, g, masked):
        causal = (j * tk + col) < qpos if masked else None
        for half in range(2):
            rows = slice(half * tq, (half + 1) * tq)
            t = z[:, half * tk:(half + 1) * tk] + b2[half]
            sp = jnp.log(1.0 + jnp.exp2(-jnp.abs(t))) * LOG2E
            lb = jnp.minimum(t, 0.0) - sp
            l1m = lb - t
            if masked:
                l1m = jnp.where(causal, l1m, 0.0)
                lb = jnp.where(causal, lb, NEG_BIG)
            lb_scr[slot, g, rows, :] = lb
            x_scr[slot, g, rows, :] = l1m.astype(BF16)

    def finish_weights(r, j, slot, g):
        suffix = r[:, :LANES] + carry_ref[...]
        carry_ref[...] += r[:, LANES:]
        a = jnp.exp2(lb_scr[slot, g] + suffix).astype(BF16)
        a2 = jnp.concatenate([a[:tq], a[tq:]], axis=1)
        acc_ref[...] += lax.dot_general(a2, vbd_ref[j], _NT, preferred_element_type=F32)

    def stage(w_blocks, slot_w, m_s, slot_s, masked=False, between=None, before=None):
        if before is not None:
            before()
        rs, zs = [], []
        for g in range(grp):
            if w_blocks is not None:
                rs.append(jnp.dot(x_scr[slot_w, g], w2_ref[...], preferred_element_type=F32))
            if m_s is not None:
                zs.append(jnp.dot(q2, kbd_ref[block_of(m_s, g)], preferred_element_type=F32))
        for g in range(grp):
            if w_blocks is not None:
                finish_weights(rs[g], w_blocks[g], slot_w, g)
        if between is not None:
            between()
        for g in range(grp):
            if m_s is not None:
                finish_scores(zs[g], block_of(m_s, g), slot_s, g, masked)

    def blocks(m):
        return [block_of(m, g) for g in range(grp)]

    last_blocks = [grp - 1 - g for g in range(grp)]

    @pl.when(qi == 0)
    def _():
        reset()
        stage(None, None, 0, pend, masked=True, before=sample)

    @pl.when(qi > 0)
    def _():
        def between():
            emit(qi - 1)
            reset()
        stage(last_blocks, pend, 0, 0, masked=True, between=between, before=sample)

    def loop_body(p, c):
        stage(blocks(2 * p), 0, 2 * p + 1, 1)
        stage(blocks(2 * p + 1), 1, 2 * p + 2, 0)
        return c

    lax.fori_loop(0, (qi - 1) // 2, loop_body, 0)

    @pl.when(qi % 2 == 1)
    def _():
        stage(blocks(qi - 1), 0, qi, pend)

    @pl.when((qi % 2 == 0) & (qi > 0))
    def _():
        stage(blocks(qi - 2), 0, qi - 1, 1)
        stage(blocks(qi - 1), 1, qi, pend)

    @pl.when(qi == pl.num_programs(2) - 1)
    def _():
        stage(last_blocks, pend, None, None)
        emit(qi)


def _sb_prompt(q_bf, kt, vt, bias, sample=None):
    tq, tk = SB_TQ, SB_TK
    batch, _, seq = kt.shape
    assert seq % tq == 0 and tq % tk == 0
    nq = seq // tq
    npairs = SB_W // LANES
    const = lambda shape: pl.BlockSpec(shape, lambda b, h, i, pt: (0,) * len(shape))
    in_specs = [pl.BlockSpec(memory_space=pltpu.SMEM),
                pl.BlockSpec((tq, LANES), lambda b, h, i, pt: (b * nq + i, h)),
                pl.BlockSpec((1, LANES, seq), lambda b, h, i, pt: (b, h, 0)),
                pl.BlockSpec((1, LANES, seq), lambda b, h, i, pt: (b, h, 0)),
                const((LANES, 2 * LANES))]
    out_specs = [pl.BlockSpec((seq, LANES), lambda b, h, i, pt: (b, h))]
    out_shape = [jax.ShapeDtypeStruct((batch * seq, SB_W), F32)]
    scratch = [pltpu.VMEM((seq // tk, LANES, 2 * tk), BF16),
               pltpu.VMEM((seq // tk, LANES, 2 * tk), BF16),
               pltpu.VMEM((2 * tq, LANES), F32),
               pltpu.VMEM((tq, LANES), F32),
               pltpu.VMEM((3, tq // tk, 2 * tq, LANES), F32),
               pltpu.VMEM((3, tq // tk, 2 * tq, LANES), BF16)]
    args = [bias, q_bf, kt, vt, _suffix_matrix()[:LANES]]
    npg, sample_steps = 0, 1
    page_table = jnp.zeros((1,), jnp.int32)
    if sample is not None:
        qs_bf, bias_s, page_table, ckt, cvt = sample
        dec_b, n_pages = page_table.shape
        npg = PAGES_PER_STEP
        sample_steps = n_pages // npg
        page = ckt.shape[2]
        assert page == LANES and n_pages % npg == 0 and dec_b * sample_steps == batch * npairs * nq
        n = npg * SB_HEADS
        lin = lambda b, h, i: (b * npairs + h) * nq + i

        def page_spec(j):
            def index(b, h, i, pt):
                row, s = lin(b, h, i) // sample_steps, lin(b, h, i) % sample_steps
                return (pt[row * n_pages + (n_pages - 1 - (s * npg + j))], 0, 0)
            return pl.BlockSpec((1, SB_W, page), index)

        row_spec = pl.BlockSpec((1, 1, SB_W), lambda b, h, i, pt: (lin(b, h, i) // sample_steps, 0, 0))
        in_specs += [row_spec, const((SB_HEADS, LANES)), const((2 * LANES, 2 * LANES)), const((2 * n, 2 * n))]
        in_specs += [page_spec(j) for j in range(npg)] * 2
        out_specs.append(row_spec)
        out_shape.append(jax.ShapeDtypeStruct((dec_b, 1, SB_W), F32))
        scratch += [pltpu.VMEM((SB_HEADS, LANES), F32), pltpu.VMEM((SB_HEADS, SB_W), F32)]
        args += [qs_bf.reshape(dec_b, 1, SB_W), jnp.broadcast_to(bias_s.astype(F32)[:, None], (SB_HEADS, LANES)),
                 _suffix_matrix(), _page_matrices()] + [ckt] * npg + [cvt] * npg
        page_table = page_table.reshape(-1)
    outs = pl.pallas_call(
        functools.partial(_sb_prompt_kernel, npg=npg, sample_steps=sample_steps),
        grid_spec=pltpu.PrefetchScalarGridSpec(num_scalar_prefetch=1, grid=(batch, npairs, nq), in_specs=in_specs,
                                               out_specs=out_specs, scratch_shapes=scratch),
        out_shape=out_shape,
        compiler_params=pltpu.CompilerParams(dimension_semantics=("arbitrary", "arbitrary", "arbitrary"),
                                             vmem_limit_bytes=SB_VMEM_LIMIT),
    )(page_table, *args)
    if sample is None:
        return outs[0], None
    return outs[0], outs[1].reshape(-1, SB_W)


def _hgrn_levels(c):
    return [c >> (i + 1) for i in range(int(math.log2(c)))]


def _hgrn_constants(c):
    t = np.arange(c)[:, None]
    u = np.arange(c)[None, :]
    rows = [(u <= t), (u > t)]
    masks = []
    for hs in _hgrn_levels(c):
        mid = (t // (2 * hs)) * (2 * hs) + hs - 1
        upper = (t % (2 * hs)) >= hs
        rows.append((upper & (u > mid) & (u <= t)) | ((~upper) & (u > t) & (u <= mid)))
        s = u
        masks.append(((t // (2 * hs)) == (s // (2 * hs))) & upper & ((s % (2 * hs)) < hs))
    masks.append(t == u)
    mall = np.concatenate(rows, axis=0).astype(np.float32)
    mall2 = np.concatenate([mall, mall], axis=1)
    return jnp.asarray(mall2, dtype=BF16), jnp.asarray(np.stack(masks).astype(np.float32))


def _forget_lower_bound(lbl_ref):
    l0 = lbl_ref[0:1]
    l1 = lbl_ref[1:2]
    m = jnp.maximum(l0, l1)
    e0 = jnp.exp(l0 - m)
    e1 = jnp.exp(l1 - m)
    return e0 / (e0 + e1)


def _hgrn_prompt_kernel(lbl_ref, hq_ref, hf_ref, hv_ref, mall_ref, bm_ref, o_ref, st_ref, s_scr, d_scr):
    c = HG_CHUNK
    ci = pl.program_id(1)
    levels = _hgrn_levels(c)
    n_sub = hq_ref.shape[1] // c

    @pl.when(ci == 0)
    def _():
        s_scr[...] = jnp.zeros_like(s_scr)

    lb = _forget_lower_bound(lbl_ref)
    lane = lax.broadcasted_iota(jnp.int32, (c, HG_W), 1)
    head_masks = [(lane >= h * HG_D) & (lane < (h + 1) * HG_D) for h in range(HG_HEADS)]
    r = lax.broadcasted_iota(jnp.int32, (HG_W, HG_W), 0)
    cc = lax.broadcasted_iota(jnp.int32, (HG_W, HG_W), 1)
    same_head = (r // HG_D) == (cc // HG_D)

    def by_head(x_bf):
        z = jnp.zeros_like(x_bf)
        return jnp.concatenate([jnp.where(m, x_bf, z) for m in head_masks], axis=0)

    def pair_scores(q_f32, k_f32, mask):
        p = lax.dot_general(q_f32.astype(BF16), by_head(k_f32.astype(BF16)), _NT, preferred_element_type=F32)
        return p * jnp.concatenate([mask] * HG_HEADS, axis=1)

    st = s_scr[...]
    for sub in range(n_sub):
        rows = slice(sub * c, (sub + 1) * c)
        d = d_scr.at[sub]
        q = hq_ref[0, rows, :]
        f = lb + (1.0 - lb) * jax.nn.sigmoid(hf_ref[0, rows, :])
        logf = jnp.log(f)
        k = 1.0 - f
        v_bf = hv_ref[0, rows, :].astype(BF16)
        hi, lo = _split_bf16(logf)
        d[...] = jnp.dot(mall_ref[...], jnp.concatenate([hi, lo], axis=0), preferred_element_type=F32)

        attn = pair_scores(q, k, bm_ref[len(levels)])
        for li in range(len(levels)):
            e = jnp.exp(d[(2 + li) * c:(3 + li) * c, :])
            attn = attn + pair_scores(q * e, k * e, bm_ref[li])
        o = jnp.dot(attn.astype(BF16), by_head(v_bf), preferred_element_type=F32)

        bcum = d[0:c, :]
        o = o + lax.dot_general((q * jnp.exp(bcum)).astype(BF16), st.astype(BF16), _NT,
                                preferred_element_type=F32)
        o_ref[0, rows, :] = o

        k_dec = (k * jnp.exp(d[c:2 * c, :])).astype(BF16)
        upd = lax.dot_general(v_bf, k_dec, _TN, preferred_element_type=F32)
        st = st * jnp.exp(bcum[c - 1:c, :]) + jnp.where(same_head, upd, 0.0)
    s_scr[...] = st

    @pl.when(ci == pl.num_programs(1) - 1)
    def _():
        st_ref[0] = st


def _hgrn_prompt(rest3, lb_logits):
    b, t, _ = rest3.shape
    c = HG_CHUNK
    mall2, bm = _hgrn_constants(c)
    nr = mall2.shape[0]
    n_sub = HG_CHUNKS_PER_STEP if t % (HG_CHUNKS_PER_STEP * c) == 0 else 1
    rows = n_sub * c
    col = lambda j: (lambda bi, ci: (bi, ci, j))
    return pl.pallas_call(
        _hgrn_prompt_kernel,
        grid=(b, t // rows),
        in_specs=[pl.BlockSpec((2, HG_W), lambda bi, ci: (0, 0)),
                  pl.BlockSpec((1, rows, HG_W), col(2)),
                  pl.BlockSpec((1, rows, HG_W), col(3)),
                  pl.BlockSpec((1, rows, HG_W), col(4)),
                  pl.BlockSpec((nr, 2 * c), lambda bi, ci: (0, 0)),
                  pl.BlockSpec(bm.shape, lambda bi, ci: (0, 0, 0))],
        out_specs=[pl.BlockSpec((1, rows, HG_W), lambda bi, ci: (bi, ci, 0)),
                   pl.BlockSpec((1, HG_W, HG_W), lambda bi, ci: (bi, 0, 0))],
        out_shape=[jax.ShapeDtypeStruct((b, t, HG_W), F32), jax.ShapeDtypeStruct((b, HG_W, HG_W), F32)],
        scratch_shapes=[pltpu.VMEM((HG_W, HG_W), F32), pltpu.VMEM((n_sub, nr, HG_W), F32)],
        compiler_params=pltpu.CompilerParams(dimension_semantics=("arbitrary", "arbitrary"),
                                             vmem_limit_bytes=VMEM_LIMIT),
    )(lb_logits, rest3, rest3, rest3, mall2, bm)


def _silu(g):
    return g * jax.nn.sigmoid(g)


def _mix_out_kernel(x_ref, sbo_ref, hgo_ref, sbg_ref, hgg_ref, xaq_ref, xag_ref, mkt_ref, mvt_ref,
                    wout_ref, hgn_ref, bd2_ref, fng_ref, y_ref):
    tm = x_ref.shape[1]
    rows = max(tm, 8)

    def load(ref):
        a = ref[0]
        return jnp.broadcast_to(a, (rows, a.shape[-1])) if tm < rows else a

    x = load(x_ref)
    lane = lax.broadcasted_iota(jnp.int32, (rows, XA_W), 1)
    row_m = lax.broadcasted_iota(jnp.int32, (XA_W, mkt_ref.shape[2]), 0)

    xq = (load(xaq_ref) * (1.0 / math.sqrt(XA_DH))).astype(BF16)
    zq = jnp.zeros_like(xq)
    q_st = jnp.concatenate([jnp.where((lane >= h * XA_DH) & (lane < (h + 1) * XA_DH), xq, zq)
                            for h in range(XA_HEADS)], axis=0)
    s = jnp.dot(q_st, mkt_ref[0].astype(BF16), preferred_element_type=F32)
    p = jnp.exp(s - jnp.max(s, axis=-1, keepdims=True))
    p = (p / jnp.sum(p, axis=-1, keepdims=True)).astype(BF16)
    p2 = jnp.concatenate([p[h * rows:(h + 1) * rows] for h in range(XA_HEADS)], axis=1)
    mvt = mvt_ref[0].astype(BF16)
    zv = jnp.zeros_like(mvt)
    mvt_st = jnp.concatenate([jnp.where((row_m >= h * XA_DH) & (row_m < (h + 1) * XA_DH), mvt, zv)
                              for h in range(XA_HEADS)], axis=1)
    xa_o = lax.dot_general(p2, mvt_st, _NT, preferred_element_type=F32)

    hg = load(hgo_ref)
    hi, lo = _split_bf16(hg * hg)
    ms = jnp.dot(jnp.concatenate([hi, lo], axis=1), bd2_ref[...], preferred_element_type=F32) * (1.0 / HG_D)
    hg_n = (hg * lax.rsqrt(ms + EPS)) * hgn_ref[...]

    sb = (load(sbo_ref) * _silu(load(sbg_ref))).astype(BF16)
    hgm = (hg_n * _silu(load(hgg_ref))).astype(BF16)
    xam = (xa_o * _silu(load(xag_ref))).astype(BF16)
    y = x + jnp.dot(sb, wout_ref[0:SB_W, :], preferred_element_type=F32)
    y = y + jnp.dot(hgm, wout_ref[SB_W:SB_W + HG_W, :], preferred_element_type=F32)
    y = y + jnp.dot(xam, wout_ref[SB_W + HG_W:, :], preferred_element_type=F32)
    msy = jnp.mean(y * y, axis=-1, keepdims=True)
    y = (y * lax.rsqrt(msy + EPS)) * fng_ref[...]
    y_ref[0] = y[0:tm]


def _mix_out(x3, sbo3, hgo3, rest3, mkt3, mvt3, wout_bf, hg_norm_gain, final_gain, tm):
    b, t, d = x3.shape
    nm = mkt3.shape[2]
    bd = (np.arange(HG_W)[:, None] // HG_D) == (np.arange(HG_W)[None, :] // HG_D)
    bd2 = jnp.asarray(np.concatenate([bd, bd], axis=0).astype(np.float32), dtype=BF16)
    blk = lambda w, j: pl.BlockSpec((1, tm, w), lambda bi, ti: (bi, ti, j))
    const = lambda shape: pl.BlockSpec(shape, lambda bi, ti: (0,) * len(shape))
    return pl.pallas_call(
        _mix_out_kernel,
        grid=(b, t // tm),
        in_specs=[blk(d, 0), blk(SB_W, 0), blk(HG_W, 0),
                  blk(SB_W, 0),
                  blk(HG_W, 5),
                  blk(XA_W, 6),
                  blk(XA_W, 7),
                  pl.BlockSpec((1, XA_W, nm), lambda bi, ti: (bi, 0, 0)),
                  pl.BlockSpec((1, XA_W, nm), lambda bi, ti: (bi, 0, 0)),
                  const(wout_bf.shape), const((1, HG_W)), const(bd2.shape), const((1, d))],
        out_specs=blk(d, 0),
        out_shape=jax.ShapeDtypeStruct((b, t, d), F32),
        compiler_params=pltpu.CompilerParams(dimension_semantics=("arbitrary", "arbitrary"),
                                             vmem_limit_bytes=VMEM_LIMIT),
    )(x3, sbo3, hgo3, rest3, rest3, rest3, rest3, mkt3, mvt3, wout_bf,
      hg_norm_gain.reshape(1, HG_W), bd2, final_gain.reshape(1, d))


def _page_matrices():
    n = PAGES_PER_STEP * SB_HEADS
    r = np.arange(n)[:, None]
    c = np.arange(n)[None, :]
    same_head = (r % SB_HEADS) == (c % SB_HEADS)
    before = same_head & ((c // SB_HEADS) < (r // SB_HEADS))
    mats = [np.concatenate([m, m], axis=1).astype(np.float32) for m in (before, same_head)]
    return jnp.asarray(np.concatenate(mats, axis=0), dtype=BF16)


def _sb_sample_step(first, q_ref, bias_ref, w2_ref, pm_ref, kt_refs, vt_refs, o_ref, carry_ref, acc_ref):
    npg = len(kt_refs)
    n = npg * SB_HEADS
    row8 = lax.broadcasted_iota(jnp.int32, (SB_HEADS, SB_W), 0)
    lane8 = lax.broadcasted_iota(jnp.int32, (SB_HEADS, SB_W), 1)
    own = (lane8 // SB_DH) == row8
    qb = jnp.broadcast_to(q_ref[0].astype(F32), (SB_HEADS, SB_W))
    q_bd = jnp.where(own, qb, 0.0).astype(BF16)

    z = jnp.concatenate([jnp.dot(q_bd, kt_refs[i][0].astype(BF16), preferred_element_type=F32)
                         for i in range(npg)], axis=0)
    bias_t = jnp.concatenate([bias_ref[...]] * npg, axis=0)
    log_beta, log_1m = _log_sigmoid_pair(z + bias_t)
    hi, lo = _split_bf16(log_1m)
    r = jnp.dot(jnp.concatenate([hi, lo], axis=1), w2_ref[...], preferred_element_type=F32)
    tot = r[:, LANES:]
    thi, tlo = _split_bf16(tot)
    cross = jnp.dot(pm_ref[...], jnp.concatenate([thi, tlo], axis=0), preferred_element_type=F32)
    carry = jnp.where(first, 0.0, carry_ref[...])
    suffix = r[:, :LANES] + cross[:n] + jnp.concatenate([carry] * npg, axis=0)
    carry_ref[...] = carry + cross[n:n + SB_HEADS]
    a = jnp.exp(log_beta + suffix).astype(BF16)
    acc = jnp.where(first, 0.0, acc_ref[...])
    for i in range(npg):
        acc = acc + lax.dot_general(a[i * SB_HEADS:(i + 1) * SB_HEADS], vt_refs[i][0].astype(BF16), _NT,
                                    preferred_element_type=F32)
    acc_ref[...] = acc
    o_ref[0] = jnp.sum(jnp.where(own, acc, 0.0), axis=0, keepdims=True)


def _sb_sample_kernel(pt_ref, q_ref, bias_ref, w2_ref, pm_ref, *refs):
    npg = PAGES_PER_STEP
    o_ref, carry_ref, acc_ref = refs[2 * npg:]

    @pl.when((pl.program_id(0) == 0) & (pl.program_id(1) == 0))
    def _():
        carry_ref[...] = jnp.zeros_like(carry_ref)
        acc_ref[...] = jnp.zeros_like(acc_ref)

    _sb_sample_step(pl.program_id(1) == 0, q_ref, bias_ref, w2_ref, pm_ref, refs[:npg], refs[npg:2 * npg],
                    o_ref, carry_ref, acc_ref)


def _sb_sample(q_bf, bias, page_table, ckt, cvt):
    b = q_bf.shape[0]
    n_pages = page_table.shape[1]
    page = ckt.shape[2]
    assert page == LANES and n_pages % PAGES_PER_STEP == 0
    npg = PAGES_PER_STEP
    nsteps = n_pages // npg

    def page_spec(i):
        return pl.BlockSpec((1, SB_W, page),
                            lambda bi, si, pt: (pt[bi * n_pages + (n_pages - 1 - (si * npg + i))], 0, 0))

    n = npg * SB_HEADS
    grid_spec = pltpu.PrefetchScalarGridSpec(
        num_scalar_prefetch=1,
        grid=(b, nsteps),
        in_specs=[pl.BlockSpec((1, 1, SB_W), lambda bi, si, pt: (bi, 0, 0)),
                  pl.BlockSpec((SB_HEADS, LANES), lambda bi, si, pt: (0, 0)),
                  pl.BlockSpec((2 * LANES, 2 * LANES), lambda bi, si, pt: (0, 0)),
                  pl.BlockSpec((2 * n, 2 * n), lambda bi, si, pt: (0, 0))]
                 + [page_spec(i) for i in range(npg)] * 2,
        out_specs=pl.BlockSpec((1, 1, SB_W), lambda bi, si, pt: (bi, 0, 0)),
        scratch_shapes=[pltpu.VMEM((SB_HEADS, LANES), F32), pltpu.VMEM((SB_HEADS, SB_W), F32)])
    bias_rep = jnp.broadcast_to(bias.astype(F32)[:, None], (SB_HEADS, LANES))
    out = pl.pallas_call(
        _sb_sample_kernel,
        grid_spec=grid_spec,
        out_shape=jax.ShapeDtypeStruct((b, 1, SB_W), F32),
        compiler_params=pltpu.CompilerParams(dimension_semantics=("arbitrary", "arbitrary"),
                                             vmem_limit_bytes=VMEM_LIMIT),
    )(page_table.reshape(-1), q_bf.reshape(b, 1, SB_W), bias_rep, _suffix_matrix(), _page_matrices(),
      *([ckt] * npg), *([cvt] * npg))
    return out.reshape(b, SB_W)


def _hgrn_sample_kernel(lbl_ref, s_ref, q_ref, f_ref, v_ref, o_ref, sn_ref):
    l0 = lbl_ref[0]
    l1 = lbl_ref[1]
    m = jnp.maximum(l0, l1)
    e0 = jnp.exp(l0 - m)
    e1 = jnp.exp(l1 - m)
    lb = e0 / (e0 + e1)
    f = lb + (1.0 - lb) * jax.nn.sigmoid(f_ref[0])
    k = 1.0 - f
    v = v_ref[0]
    v_rows = jnp.concatenate([jnp.broadcast_to(v[:, h * HG_D:(h + 1) * HG_D], (HG_D, HG_D))
                              for h in range(HG_HEADS)], axis=0)
    s_new = f * s_ref[0] + k * v_rows
    sn_ref[0] = s_new
    qs = q_ref[0] * s_new
    o_ref[0] = jnp.sum(qs.reshape(HG_HEADS, HG_D, HG_D), axis=1)


def _hgrn_sample(state, q_col, f_col, v_row, lb_logits):
    b = state.shape[0]
    row = lambda shape: pl.BlockSpec((1,) + shape, lambda bi: (bi, 0, 0))
    return pl.pallas_call(
        _hgrn_sample_kernel,
        grid=(b,),
        in_specs=[pl.BlockSpec((2, HG_W, 1), lambda bi: (0, 0, 0)),
                  row((HG_W, HG_D)), row((HG_W, 1)), row((HG_W, 1)), row((1, HG_W))],
        out_specs=[row((HG_HEADS, HG_D)), row((HG_W, HG_D))],
        out_shape=[jax.ShapeDtypeStruct((b, HG_HEADS, HG_D), F32), jax.ShapeDtypeStruct((b, HG_W, HG_D), F32)],
        compiler_params=pltpu.CompilerParams(dimension_semantics=("arbitrary",)),
    )(lb_logits.reshape(2, HG_W, 1), state, q_col, f_col, v_row)


def _in_proj_outs(transposed_kv):
    qs = (LOG2E if transposed_kv else 1.0) / math.sqrt(SB_DH)
    q_out = (0, SB_W, BF16, qs)
    rest_out = (3 * SB_W, 3 * SB_W + D_REST, F32, 1.0)
    if transposed_kv:
        return [q_out, rest_out]
    return [q_out, (SB_W, 2 * SB_W, F32, 1.0), (2 * SB_W, 3 * SB_W, F32, 1.0), rest_out]


def kernel(x_prompt, x_sample, mem_prompt, cache_k, cache_v, page_table, state_hgrn, cache_mem_k, cache_mem_v,
           norm_gain, w_in, sb_bias, hg_lb_logits, hg_norm_gain, mem_norm_gain, w_mem_kv, w_out, final_norm_gain):
    batch, seq, d = x_prompt.shape
    dec_b = x_sample.shape[0]
    n_mem = mem_prompt.shape[1]
    depth = w_in.shape[0]
    assert depth == 1 and x_sample.shape[1] == 1
    l = 0
    lb_logits = hg_lb_logits.astype(F32)
    assert lb_logits.shape[0] == 2
    w_in_bf = w_in[l].astype(BF16)
    w_out_bf = w_out[l].astype(BF16)
    w_mem_bf = w_mem_kv[l].astype(BF16)
    bias = sb_bias[l].astype(F32)

    w_kv_t = w_in[l][:, SB_W:3 * SB_W].T.astype(BF16)
    q_bf, rest, kt_p, vt_p = _norm_proj(x_prompt, norm_gain[l], w_in_bf, _in_proj_outs(True), tm=256,
                                        wt_bf=w_kv_t, t_widths=(SB_W, SB_W))
    mkt, mvt = _norm_proj(mem_prompt, mem_norm_gain[l], w_mem_bf, [], tm=256,
                          wt_bf=w_mem_kv[l].T.astype(BF16), t_widths=(XA_W, XA_W))
    qs_bf, k_s, v_s, rest_s = _norm_proj(x_sample.reshape(1, dec_b, d), norm_gain[l], w_in_bf,
                                         _in_proj_outs(False), tm=dec_b)

    n_pool, page = cache_k.shape[1], cache_k.shape[2]
    ckt = jnp.transpose(cache_k[l], (0, 2, 3, 1)).reshape(n_pool, SB_W, page)
    cvt = jnp.transpose(cache_v[l], (0, 2, 3, 1)).reshape(n_pool, SB_W, page)
    n_pages = page_table.shape[1]
    fused = (n_pages % PAGES_PER_STEP == 0 and
             dec_b * (n_pages // PAGES_PER_STEP) == batch * (SB_W // LANES) * (seq // SB_TQ))
    if fused:
        sb_o, sb_o_s = _sb_prompt(q_bf, kt_p, vt_p, bias, sample=(qs_bf, bias, page_table, ckt, cvt))
    else:
        sb_o, _ = _sb_prompt(q_bf, kt_p, vt_p, bias)
        sb_o_s = _sb_sample(qs_bf, bias, page_table, ckt, cvt)

    rest3 = rest.reshape(batch, seq, D_REST)
    hg_o, st_t = _hgrn_prompt(rest3, lb_logits)
    y_prompt = _mix_out(x_prompt, sb_o.reshape(batch, seq, SB_W), hg_o, rest3, mkt, mvt,
                        w_out_bf, hg_norm_gain[l], final_norm_gain, tm=512)
    hgrn_prompt = jnp.stack([st_t[:, h * HG_D:(h + 1) * HG_D, h * HG_D:(h + 1) * HG_D]
                             for h in range(HG_HEADS)], axis=1).swapaxes(-1, -2)

    hq_s = rest_s[:, SB_W:SB_W + HG_W].reshape(dec_b, HG_W, 1)
    hf_s = rest_s[:, SB_W + HG_W:SB_W + 2 * HG_W].reshape(dec_b, HG_W, 1)
    hv_s = rest_s[:, SB_W + 2 * HG_W:SB_W + 3 * HG_W].reshape(dec_b, 1, HG_W)
    hg_o_s, st_s = _hgrn_sample(state_hgrn[l].astype(F32).reshape(dec_b, HG_W, HG_D), hq_s, hf_s, hv_s, lb_logits)
    y_sample = _mix_out(x_sample, sb_o_s.reshape(dec_b, 1, SB_W), hg_o_s.reshape(dec_b, 1, HG_W),
                        rest_s.reshape(dec_b, 1, D_REST),
                        jnp.transpose(cache_mem_k[l], (0, 2, 3, 1)).reshape(dec_b, XA_W, n_mem),
                        jnp.transpose(cache_mem_v[l], (0, 2, 3, 1)).reshape(dec_b, XA_W, n_mem),
                        w_out_bf, hg_norm_gain[l], final_norm_gain, tm=1)

    def token_major(a_t, heads, dh):
        b_, _, t_ = a_t.shape
        return jnp.transpose(a_t.reshape(b_, heads, dh, t_), (0, 3, 1, 2))[None]

    return (y_prompt, y_sample, token_major(kt_p, SB_HEADS, SB_DH), token_major(vt_p, SB_HEADS, SB_DH),
            hgrn_prompt.reshape(1, batch, HG_HEADS, HG_D, HG_D),
            token_major(mkt, XA_HEADS, XA_DH), token_major(mvt, XA_HEADS, XA_DH),
            k_s.reshape(1, dec_b, 1, SB_HEADS, SB_DH), v_s.reshape(1, dec_b, 1, SB_HEADS, SB_DH),
            st_s.reshape(1, dec_b, HG_HEADS, HG_D, HG_D).astype(state_hgrn.dtype))
```

```python
import functools
import math

import numpy as np
import jax
import jax.numpy as jnp
from jax import lax
from jax.experimental import pallas as pl
from jax.experimental.pallas import tpu as pltpu

F32 = jnp.float32
BF16 = jnp.bfloat16
EPS = 1e-6

SB_HEADS, SB_DH = 8, 64
HG_HEADS, HG_D = 4, 64
XA_HEADS, XA_DH = 4, 64
SB_W = SB_HEADS * SB_DH
HG_W = HG_HEADS * HG_D
XA_W = XA_HEADS * XA_DH
D_REST = SB_W + 3 * HG_W + HG_W + 2 * XA_W

LANES = 128
SB_TQ = 512
SB_TK = 128
NEG_BIG = -1e30
LOG2E = math.log2(math.e)
HG_CHUNK = 128
HG_CHUNKS_PER_STEP = 4
PAGES_PER_STEP = 16
PAGES_PER_DOT = 4
VMEM_LIMIT = 48 * 1024 * 1024
SB_VMEM_LIMIT = 56 * 1024 * 1024

_NT = (((1,), (1,)), ((), ()))
_TN = (((0,), (0,)), ((), ()))


def _split_bf16(x):
    hi = x.astype(BF16)
    lo = (x - hi.astype(F32)).astype(BF16)
    return hi, lo


def _log_sigmoid_pair(t):
    sp = jnp.log1p(jnp.exp(-jnp.abs(t)))
    log_beta = jnp.minimum(t, 0.0) - sp
    return log_beta, log_beta - t


def _norm_proj_kernel(x_ref, g_ref, w_ref, *refs, outs, t_widths):
    wt_ref, out_refs = (refs[0], refs[1:]) if t_widths else (None, refs)
    x = x_ref[...]
    ms = jnp.mean(x * x, axis=-1, keepdims=True)
    xn = ((x * lax.rsqrt(ms + EPS)) * g_ref[...]).astype(BF16)
    cache = {}
    for o_ref, (lo, hi, scale) in zip(out_refs, outs):
        if (lo, hi) not in cache:
            cache[(lo, hi)] = jnp.dot(xn, w_ref[:, lo:hi], preferred_element_type=F32)
        h = cache[(lo, hi)]
        if scale != 1.0:
            h = h * scale
        o_ref[...] = h.astype(o_ref.dtype)
    off = 0
    for o_ref, width in zip(out_refs[len(outs):], t_widths):
        o_ref[0] = lax.dot_general(wt_ref[off:off + width, :], xn, _NT, preferred_element_type=F32)
        off += width


def _norm_proj(x3, gain, w_bf, outs, tm, wt_bf=None, t_widths=()):
    b, t, d = x3.shape
    m = b * t
    n = w_bf.shape[1]
    tm = min(tm, m)
    assert (t % tm == 0 or tm == m) and m % tm == 0
    nbt = max(t // tm, 1)
    assert (wt_bf is None) == (len(t_widths) == 0)
    t_args = [] if wt_bf is None else [wt_bf]
    kern = functools.partial(_norm_proj_kernel, outs=[(lo, hi, sc) for lo, hi, _, sc in outs], t_widths=tuple(t_widths))
    return pl.pallas_call(
        kern,
        grid=(m // tm,),
        in_specs=[pl.BlockSpec((tm, d), lambda i: (i, 0)),
                  pl.BlockSpec((1, d), lambda i: (0, 0)),
                  pl.BlockSpec((d, n), lambda i: (0, 0))]
                 + [pl.BlockSpec(a.shape, lambda i: (0, 0)) for a in t_args],
        out_specs=[pl.BlockSpec((tm, hi - lo), lambda i: (i, 0)) for lo, hi, _, _ in outs]
                  + [pl.BlockSpec((1, w, tm), lambda i: (i // nbt, 0, i % nbt)) for w in t_widths],
        out_shape=[jax.ShapeDtypeStruct((m, hi - lo), dt) for lo, hi, dt, _ in outs]
                  + [jax.ShapeDtypeStruct((b, w, t), F32) for w in t_widths],
        compiler_params=pltpu.CompilerParams(dimension_semantics=("arbitrary",), vmem_limit_bytes=VMEM_LIMIT),
    )(x3.reshape(m, d), gain.reshape(1, d), w_bf, *t_args)


def _suffix_matrix():
    jp = np.arange(LANES)[:, None]
    j = np.arange(LANES)[None, :]
    w = np.concatenate([(jp > j).astype(np.float32), np.ones((LANES, LANES), np.float32)], axis=1)
    return jnp.asarray(np.concatenate([w, w], axis=0), dtype=BF16)


def _sb_prompt_kernel(pt_ref, bias_ref, q_ref, kt_ref, vt_ref, w2_ref, *refs, npg, sample_steps):
    tq, tk = SB_TQ, SB_TK
    hp = pl.program_id(1)
    qi = pl.program_id(2)
    nkb = kt_ref.shape[2] // tk
    if npg:
        qs_ref, bias_s_ref, w2s_ref, pm_ref = refs[:4]
        page_refs = refs[4:4 + 2 * npg]
        o_ref, os_ref, kbd_ref, vbd_ref, carry_ref, acc_ref, lb_scr, x_scr, carry_s_ref, acc_s_ref = refs[4 + 2 * npg:]
        lin = (pl.program_id(0) * pl.num_programs(1) + hp) * pl.num_programs(2) + qi

        @pl.when(lin == 0)
        def _():
            carry_s_ref[...] = jnp.zeros_like(carry_s_ref)
            acc_s_ref[...] = jnp.zeros_like(acc_s_ref)

        sample = _sb_sample_parts(lin % sample_steps == 0, qs_ref, bias_s_ref, w2s_ref, pm_ref,
                                  page_refs[:npg], page_refs[npg:], os_ref, carry_s_ref, acc_s_ref)
    else:
        o_ref, kbd_ref, vbd_ref, carry_ref, acc_ref, lb_scr, x_scr = refs
        sample = None

    @pl.when(qi == 0)
    def _():
        row_v = lax.broadcasted_iota(jnp.int32, (LANES, tk), 0)
        for j in range(nkb):
            for src, dst in ((kt_ref, kbd_ref), (vt_ref, vbd_ref)):
                blk = src[0, :, j * tk:(j + 1) * tk].astype(BF16)
                zero = jnp.zeros_like(blk)
                dst[j, :, 0:tk] = jnp.where(row_v < SB_DH, blk, zero)
                dst[j, :, tk:2 * tk] = jnp.where(row_v >= SB_DH, blk, zero)

    q2 = q_ref[...]
    b2 = (bias_ref[2 * hp] * LOG2E, bias_ref[2 * hp + 1] * LOG2E)
    qpos = qi * tq + lax.broadcasted_iota(jnp.int32, (tq, LANES), 0)
    col = lax.broadcasted_iota(jnp.int32, (tq, LANES), 1)

    grp = tq // tk
    pend = 2

    def block_of(m, g):
        return (qi - m) * grp + (grp - 1 - g)

    def reset():
        carry_ref[...] = jnp.zeros_like(carry_ref)
        acc_ref[...] = jnp.zeros_like(acc_ref)

    def emit(tile):
        o_ref[pl.ds(pl.multiple_of(tile * tq, tq), tq), :] = acc_ref[...]

    def finish_scores(z, j, slot, g, masked):
        causal = (j * tk + col) < qpos if masked else None
        for half in range(2):
            rows = slice(half * tq, (half + 1) * tq)
            t = z[:, half * tk:(half + 1) * tk] + b2[half]
            sp = jnp.log(1.0 + jnp.exp2(-jnp.abs(t))) * LOG2E
            lb = jnp.minimum(t, 0.0) - sp
            l1m = lb - t
            if masked:
                l1m = jnp.where(causal, l1m, 0.0)
                lb = jnp.where(causal, lb, NEG_BIG)
            lb_scr[slot, g, rows, :] = lb
            x_scr[slot, g, rows, :] = l1m.astype(BF16)

    def finish_weights(r, j, slot, g):
        suffix = r[:, :LANES] + carry_ref[...]
        carry_ref[...] += r[:, LANES:]
        a = jnp.exp2(lb_scr[slot, g] + suffix).astype(BF16)
        a2 = jnp.concatenate([a[:tq], a[tq:]], axis=1)
        acc_ref[...] += lax.dot_general(a2, vbd_ref[j], _NT, preferred_element_type=F32)

    def stage(w_blocks, slot_w, m_s, slot_s, masked=False, between=None, rider=None):
        rs, zs = [], []
        for g in range(grp):
            if w_blocks is not None:
                rs.append(jnp.dot(x_scr[slot_w, g], w2_ref[...], preferred_element_type=F32))
            if m_s is not None:
                zs.append(jnp.dot(q2, kbd_ref[block_of(m_s, g)], preferred_element_type=F32))
        if rider is not None:
            n_pieces, r_scores, r_weights, r_begin, r_values, r_end = rider
            share = lambda g: range(g * n_pieces // grp, (g + 1) * n_pieces // grp)
            rz = []
        for g in range(grp):
            if w_blocks is not None:
                finish_weights(rs[g], w_blocks[g], slot_w, g)
            if rider is not None:
                for j in share(g):
                    rz += r_scores(j)
        if between is not None:
            between()
        if rider is not None:
            ra = r_weights(rz)
            racc = r_begin()
        for g in range(grp):
            if m_s is not None:
                finish_scores(zs[g], block_of(m_s, g), slot_s, g, masked)
            if rider is not None:
                for j in share(g):
                    racc = r_values(j, ra, racc)
        if rider is not None:
            r_end(racc)

    def blocks(m):
        return [block_of(m, g) for g in range(grp)]

    last_blocks = [grp - 1 - g for g in range(grp)]

    @pl.when(qi == 0)
    def _():
        reset()
        stage(None, None, 0, pend, masked=True, rider=sample)

    @pl.when(qi > 0)
    def _():
        def between():
            emit(qi - 1)
            reset()
        stage(last_blocks, pend, 0, 0, masked=True, between=between, rider=sample)

    def loop_body(p, c):
        stage(blocks(2 * p), 0, 2 * p + 1, 1)
        stage(blocks(2 * p + 1), 1, 2 * p + 2, 0)
        return c

    lax.fori_loop(0, (qi - 1) // 2, loop_body, 0)

    @pl.when(qi % 2 == 1)
    def _():
        stage(blocks(qi - 1), 0, qi, pend)

    @pl.when((qi % 2 == 0) & (qi > 0))
    def _():
        stage(blocks(qi - 2), 0, qi - 1, 1)
        stage(blocks(qi - 1), 1, qi, pend)

    @pl.when(qi == pl.num_programs(2) - 1)
    def _():
        stage(last_blocks, pend, None, None)
        emit(qi)


def _sb_prompt(q_bf, kt, vt, bias, sample=None):
    tq, tk = SB_TQ, SB_TK
    batch, _, seq = kt.shape
    assert seq % tq == 0 and tq % tk == 0
    nq = seq // tq
    npairs = SB_W // LANES
    const = lambda shape: pl.BlockSpec(shape, lambda b, h, i, pt: (0,) * len(shape))
    in_specs = [pl.BlockSpec(memory_space=pltpu.SMEM),
                pl.BlockSpec((tq, LANES), lambda b, h, i, pt: (b * nq + i, h)),
                pl.BlockSpec((1, LANES, seq), lambda b, h, i, pt: (b, h, 0)),
                pl.BlockSpec((1, LANES, seq), lambda b, h, i, pt: (b, h, 0)),
                const((LANES, 2 * LANES))]
    out_specs = [pl.BlockSpec((seq, LANES), lambda b, h, i, pt: (b, h))]
    out_shape = [jax.ShapeDtypeStruct((batch * seq, SB_W), F32)]
    scratch = [pltpu.VMEM((seq // tk, LANES, 2 * tk), BF16),
               pltpu.VMEM((seq // tk, LANES, 2 * tk), BF16),
               pltpu.VMEM((2 * tq, LANES), F32),
               pltpu.VMEM((tq, LANES), F32),
               pltpu.VMEM((3, tq // tk, 2 * tq, LANES), F32),
               pltpu.VMEM((3, tq // tk, 2 * tq, LANES), BF16)]
    args = [bias, q_bf, kt, vt, _suffix_matrix()[:LANES]]
    npg, sample_steps = 0, 1
    page_table = jnp.zeros((1,), jnp.int32)
    if sample is not None:
        qs_bf, bias_s, page_table, ckt, cvt = sample
        dec_b, n_pages = page_table.shape
        npg = PAGES_PER_STEP
        sample_steps = n_pages // npg
        page = ckt.shape[2]
        assert page == LANES and n_pages % npg == 0 and dec_b * sample_steps == batch * npairs * nq
        n = npg * SB_HEADS
        lin = lambda b, h, i: (b * npairs + h) * nq + i

        def page_spec(j):
            def index(b, h, i, pt):
                row, s = lin(b, h, i) // sample_steps, lin(b, h, i) % sample_steps
                return (pt[row * n_pages + (n_pages - 1 - (s * npg + j))], 0, 0)
            return pl.BlockSpec((1, SB_W, page), index)

        row_spec = pl.BlockSpec((1, 1, SB_W), lambda b, h, i, pt: (lin(b, h, i) // sample_steps, 0, 0))
        in_specs += [row_spec, const((SB_HEADS, LANES)), const((2 * LANES, 2 * LANES)), const((2 * n, 2 * n))]
        in_specs += [page_spec(j) for j in range(npg)] * 2
        out_specs.append(pl.BlockSpec((1, SB_W, 1), lambda b, h, i, pt: (lin(b, h, i) // sample_steps, 0, 0)))
        out_shape.append(jax.ShapeDtypeStruct((dec_b, SB_W, 1), F32))
        scratch += [pltpu.VMEM((SB_HEADS, LANES), F32), pltpu.VMEM((SB_W, LANES), F32)]
        args += [qs_bf.reshape(dec_b, 1, SB_W), jnp.broadcast_to(bias_s.astype(F32)[:, None], (SB_HEADS, LANES)),
                 _suffix_matrix(), _page_matrices()] + [ckt] * npg + [cvt] * npg
        page_table = page_table.reshape(-1)
    outs = pl.pallas_call(
        functools.partial(_sb_prompt_kernel, npg=npg, sample_steps=sample_steps),
        grid_spec=pltpu.PrefetchScalarGridSpec(num_scalar_prefetch=1, grid=(batch, npairs, nq), in_specs=in_specs,
                                               out_specs=out_specs, scratch_shapes=scratch),
        out_shape=out_shape,
        compiler_params=pltpu.CompilerParams(dimension_semantics=("arbitrary", "arbitrary", "arbitrary"),
                                             vmem_limit_bytes=SB_VMEM_LIMIT),
    )(page_table, *args)
    if sample is None:
        return outs[0], None
    return outs[0], outs[1].reshape(-1, SB_W)


def _hgrn_levels(c):
    return [c >> (i + 1) for i in range(int(math.log2(c)))]


def _hgrn_constants(c):
    t = np.arange(c)[:, None]
    u = np.arange(c)[None, :]
    rows = [(u <= t), (u > t)]
    masks = []
    for hs in _hgrn_levels(c):
        mid = (t // (2 * hs)) * (2 * hs) + hs - 1
        upper = (t % (2 * hs)) >= hs
        rows.append((upper & (u > mid) & (u <= t)) | ((~upper) & (u > t) & (u <= mid)))
        s = u
        masks.append(((t // (2 * hs)) == (s // (2 * hs))) & upper & ((s % (2 * hs)) < hs))
    masks.append(t == u)
    mall = np.concatenate(rows, axis=0).astype(np.float32)
    mall2 = np.concatenate([mall, mall], axis=1)
    return jnp.asarray(mall2, dtype=BF16), jnp.asarray(np.stack(masks).astype(np.float32))


def _forget_lower_bound(lbl_ref):
    l0 = lbl_ref[0:1]
    l1 = lbl_ref[1:2]
    m = jnp.maximum(l0, l1)
    e0 = jnp.exp(l0 - m)
    e1 = jnp.exp(l1 - m)
    return e0 / (e0 + e1)


def _hgrn_prompt_kernel(lbl_ref, hq_ref, hf_ref, hv_ref, mall_ref, bm_ref, o_ref, st_ref, s_scr, d_scr):
    c = HG_CHUNK
    ci = pl.program_id(1)
    levels = _hgrn_levels(c)
    n_sub = hq_ref.shape[1] // c

    @pl.when(ci == 0)
    def _():
        s_scr[...] = jnp.zeros_like(s_scr)

    lb = _forget_lower_bound(lbl_ref)
    lane = lax.broadcasted_iota(jnp.int32, (c, HG_W), 1)
    head_masks = [(lane >= h * HG_D) & (lane < (h + 1) * HG_D) for h in range(HG_HEADS)]
    r = lax.broadcasted_iota(jnp.int32, (HG_W, HG_W), 0)
    cc = lax.broadcasted_iota(jnp.int32, (HG_W, HG_W), 1)
    same_head = (r // HG_D) == (cc // HG_D)

    def by_head(x_bf):
        z = jnp.zeros_like(x_bf)
        return jnp.concatenate([jnp.where(m, x_bf, z) for m in head_masks], axis=0)

    def pair_scores(q_f32, k_f32, mask):
        p = lax.dot_general(q_f32.astype(BF16), by_head(k_f32.astype(BF16)), _NT, preferred_element_type=F32)
        return p * jnp.concatenate([mask] * HG_HEADS, axis=1)

    st = s_scr[...]
    for sub in range(n_sub):
        rows = slice(sub * c, (sub + 1) * c)
        d = d_scr.at[sub]
        q = hq_ref[0, rows, :]
        f = lb + (1.0 - lb) * jax.nn.sigmoid(hf_ref[0, rows, :])
        logf = jnp.log(f)
        k = 1.0 - f
        v_bf = hv_ref[0, rows, :].astype(BF16)
        hi, lo = _split_bf16(logf)
        d[...] = jnp.dot(mall_ref[...], jnp.concatenate([hi, lo], axis=0), preferred_element_type=F32)

        attn = pair_scores(q, k, bm_ref[len(levels)])
        for li in range(len(levels)):
            e = jnp.exp(d[(2 + li) * c:(3 + li) * c, :])
            attn = attn + pair_scores(q * e, k * e, bm_ref[li])
        o = jnp.dot(attn.astype(BF16), by_head(v_bf), preferred_element_type=F32)

        bcum = d[0:c, :]
        o = o + lax.dot_general((q * jnp.exp(bcum)).astype(BF16), st.astype(BF16), _NT,
                                preferred_element_type=F32)
        o_ref[0, rows, :] = o

        k_dec = (k * jnp.exp(d[c:2 * c, :])).astype(BF16)
        upd = lax.dot_general(v_bf, k_dec, _TN, preferred_element_type=F32)
        st = st * jnp.exp(bcum[c - 1:c, :]) + jnp.where(same_head, upd, 0.0)
    s_scr[...] = st

    @pl.when(ci == pl.num_programs(1) - 1)
    def _():
        st_ref[0] = st


def _hgrn_prompt(rest3, lb_logits):
    b, t, _ = rest3.shape
    c = HG_CHUNK
    mall2, bm = _hgrn_constants(c)
    nr = mall2.shape[0]
    n_sub = HG_CHUNKS_PER_STEP if t % (HG_CHUNKS_PER_STEP * c) == 0 else 1
    rows = n_sub * c
    col = lambda j: (lambda bi, ci: (bi, ci, j))
    return pl.pallas_call(
        _hgrn_prompt_kernel,
        grid=(b, t // rows),
        in_specs=[pl.BlockSpec((2, HG_W), lambda bi, ci: (0, 0)),
                  pl.BlockSpec((1, rows, HG_W), col(2)),
                  pl.BlockSpec((1, rows, HG_W), col(3)),
                  pl.BlockSpec((1, rows, HG_W), col(4)),
                  pl.BlockSpec((nr, 2 * c), lambda bi, ci: (0, 0)),
                  pl.BlockSpec(bm.shape, lambda bi, ci: (0, 0, 0))],
        out_specs=[pl.BlockSpec((1, rows, HG_W), lambda bi, ci: (bi, ci, 0)),
                   pl.BlockSpec((1, HG_W, HG_W), lambda bi, ci: (bi, 0, 0))],
        out_shape=[jax.ShapeDtypeStruct((b, t, HG_W), F32), jax.ShapeDtypeStruct((b, HG_W, HG_W), F32)],
        scratch_shapes=[pltpu.VMEM((HG_W, HG_W), F32), pltpu.VMEM((n_sub, nr, HG_W), F32)],
        compiler_params=pltpu.CompilerParams(dimension_semantics=("arbitrary", "arbitrary"),
                                             vmem_limit_bytes=VMEM_LIMIT),
    )(lb_logits, rest3, rest3, rest3, mall2, bm)


def _silu(g):
    return g * jax.nn.sigmoid(g)


def _mix_out_kernel(x_ref, sbo_ref, hgo_ref, sbg_ref, hgg_ref, xaq_ref, xag_ref, mkt_ref, mvt_ref,
                    wout_ref, hgn_ref, bd2_ref, fng_ref, y_ref):
    tm = x_ref.shape[1]
    rows = max(tm, 8)

    def load(ref):
        a = ref[0]
        return jnp.broadcast_to(a, (rows, a.shape[-1])) if tm < rows else a

    x = load(x_ref)
    lane = lax.broadcasted_iota(jnp.int32, (rows, XA_W), 1)
    row_m = lax.broadcasted_iota(jnp.int32, (XA_W, mkt_ref.shape[2]), 0)

    xq = (load(xaq_ref) * (1.0 / math.sqrt(XA_DH))).astype(BF16)
    zq = jnp.zeros_like(xq)
    q_st = jnp.concatenate([jnp.where((lane >= h * XA_DH) & (lane < (h + 1) * XA_DH), xq, zq)
                            for h in range(XA_HEADS)], axis=0)
    s = jnp.dot(q_st, mkt_ref[0].astype(BF16), preferred_element_type=F32)
    p = jnp.exp(s - jnp.max(s, axis=-1, keepdims=True))
    p = (p / jnp.sum(p, axis=-1, keepdims=True)).astype(BF16)
    p2 = jnp.concatenate([p[h * rows:(h + 1) * rows] for h in range(XA_HEADS)], axis=1)
    mvt = mvt_ref[0].astype(BF16)
    zv = jnp.zeros_like(mvt)
    mvt_st = jnp.concatenate([jnp.where((row_m >= h * XA_DH) & (row_m < (h + 1) * XA_DH), mvt, zv)
                              for h in range(XA_HEADS)], axis=1)
    xa_o = lax.dot_general(p2, mvt_st, _NT, preferred_element_type=F32)

    hg = load(hgo_ref)
    hi, lo = _split_bf16(hg * hg)
    ms = jnp.dot(jnp.concatenate([hi, lo], axis=1), bd2_ref[...], preferred_element_type=F32) * (1.0 / HG_D)
    hg_n = (hg * lax.rsqrt(ms + EPS)) * hgn_ref[...]

    sb = (load(sbo_ref) * _silu(load(sbg_ref))).astype(BF16)
    hgm = (hg_n * _silu(load(hgg_ref))).astype(BF16)
    xam = (xa_o * _silu(load(xag_ref))).astype(BF16)
    y = x + jnp.dot(sb, wout_ref[0:SB_W, :], preferred_element_type=F32)
    y = y + jnp.dot(hgm, wout_ref[SB_W:SB_W + HG_W, :], preferred_element_type=F32)
    y = y + jnp.dot(xam, wout_ref[SB_W + HG_W:, :], preferred_element_type=F32)
    msy = jnp.mean(y * y, axis=-1, keepdims=True)
    y = (y * lax.rsqrt(msy + EPS)) * fng_ref[...]
    y_ref[0] = y[0:tm]


def _mix_out(x3, sbo3, hgo3, rest3, mkt3, mvt3, wout_bf, hg_norm_gain, final_gain, tm):
    b, t, d = x3.shape
    nm = mkt3.shape[2]
    bd = (np.arange(HG_W)[:, None] // HG_D) == (np.arange(HG_W)[None, :] // HG_D)
    bd2 = jnp.asarray(np.concatenate([bd, bd], axis=0).astype(np.float32), dtype=BF16)
    blk = lambda w, j: pl.BlockSpec((1, tm, w), lambda bi, ti: (bi, ti, j))
    const = lambda shape: pl.BlockSpec(shape, lambda bi, ti: (0,) * len(shape))
    return pl.pallas_call(
        _mix_out_kernel,
        grid=(b, t // tm),
        in_specs=[blk(d, 0), blk(SB_W, 0), blk(HG_W, 0),
                  blk(SB_W, 0),
                  blk(HG_W, 5),
                  blk(XA_W, 6),
                  blk(XA_W, 7),
                  pl.BlockSpec((1, XA_W, nm), lambda bi, ti: (bi, 0, 0)),
                  pl.BlockSpec((1, XA_W, nm), lambda bi, ti: (bi, 0, 0)),
                  const(wout_bf.shape), const((1, HG_W)), const(bd2.shape), const((1, d))],
        out_specs=blk(d, 0),
        out_shape=jax.ShapeDtypeStruct((b, t, d), F32),
        compiler_params=pltpu.CompilerParams(dimension_semantics=("arbitrary", "arbitrary"),
                                             vmem_limit_bytes=VMEM_LIMIT),
    )(x3, sbo3, hgo3, rest3, rest3, rest3, rest3, mkt3, mvt3, wout_bf,
      hg_norm_gain.reshape(1, HG_W), bd2, final_gain.reshape(1, d))


def _page_matrices():
    n = PAGES_PER_STEP * SB_HEADS
    r = np.arange(n)[:, None]
    c = np.arange(n)[None, :]
    same_head = (r % SB_HEADS) == (c % SB_HEADS)
    before = same_head & ((c // SB_HEADS) < (r // SB_HEADS))
    mats = [np.concatenate([m, m], axis=1).astype(np.float32) for m in (before, same_head)]
    return jnp.asarray(np.concatenate(mats, axis=0), dtype=BF16)


def _sb_sample_parts(first, q_ref, bias_ref, w2_ref, pm_ref, kt_refs, vt_refs, o_ref, carry_ref, acc_ref):
    npg = len(kt_refs)
    n = npg * SB_HEADS
    row8 = lax.broadcasted_iota(jnp.int32, (SB_HEADS, SB_W), 0)
    lane8 = lax.broadcasted_iota(jnp.int32, (SB_HEADS, SB_W), 1)
    own = (lane8 // SB_DH) == row8

    grp = PAGES_PER_DOT
    assert npg % grp == 0

    def pages(refs, i0):
        return jnp.concatenate([refs[i][0] for i in range(i0, i0 + grp)], axis=1).astype(BF16)

    n_pieces = npg // grp

    def scores_piece(j):
        qb = jnp.broadcast_to(q_ref[0].astype(F32), (SB_HEADS, SB_W))
        q_bd = jnp.where(own, qb, 0.0).astype(BF16)
        zc = jnp.dot(q_bd, pages(kt_refs, j * grp), preferred_element_type=F32)
        return [zc[:, i * LANES:(i + 1) * LANES] for i in range(grp)]

    def weights(z_pages):
        z = jnp.concatenate(z_pages, axis=0)
        bias_t = jnp.concatenate([bias_ref[...]] * npg, axis=0)
        log_beta, log_1m = _log_sigmoid_pair(z + bias_t)
        hi, lo = _split_bf16(log_1m)
        r = jnp.dot(jnp.concatenate([hi, lo], axis=1), w2_ref[...], preferred_element_type=F32)
        tot = r[:, LANES:]
        thi, tlo = _split_bf16(tot)
        cross = jnp.dot(pm_ref[...], jnp.concatenate([thi, tlo], axis=0), preferred_element_type=F32)
        carry = jnp.where(first, 0.0, carry_ref[...])
        suffix = r[:, :LANES] + cross[:n] + jnp.concatenate([carry] * npg, axis=0)
        carry_ref[...] = carry + cross[n:n + SB_HEADS]
        return jnp.exp(log_beta + suffix)

    def values_begin():
        return jnp.where(first, 0.0, acc_ref[...])

    def values_piece(j, a, acc):
        for i in range(j * grp, (j + 1) * grp):
            a_rows = jnp.concatenate([jnp.broadcast_to(a[i * SB_HEADS + h:i * SB_HEADS + h + 1, :], (SB_DH, LANES))
                                      for h in range(SB_HEADS)], axis=0)
            acc = acc + vt_refs[i][0] * a_rows
        return acc

    def values_end(acc):
        acc_ref[...] = acc
        o_ref[0] = jnp.sum(acc, axis=1, keepdims=True)

    return n_pieces, scores_piece, weights, values_begin, values_piece, values_end


def _run_sample_parts(parts):
    n_pieces, scores_piece, weights, values_begin, values_piece, values_end = parts
    z = []
    for j in range(n_pieces):
        z += scores_piece(j)
    a = weights(z)
    acc = values_begin()
    for j in range(n_pieces):
        acc = values_piece(j, a, acc)
    values_end(acc)


def _sb_sample_kernel(pt_ref, q_ref, bias_ref, w2_ref, pm_ref, *refs):
    npg = PAGES_PER_STEP
    o_ref, carry_ref, acc_ref = refs[2 * npg:]

    @pl.when((pl.program_id(0) == 0) & (pl.program_id(1) == 0))
    def _():
        carry_ref[...] = jnp.zeros_like(carry_ref)
        acc_ref[...] = jnp.zeros_like(acc_ref)

    _run_sample_parts(_sb_sample_parts(pl.program_id(1) == 0, q_ref, bias_ref, w2_ref, pm_ref,
                                       refs[:npg], refs[npg:2 * npg], o_ref, carry_ref, acc_ref))


def _sb_sample(q_bf, bias, page_table, ckt, cvt):
    b = q_bf.shape[0]
    n_pages = page_table.shape[1]
    page = ckt.shape[2]
    assert page == LANES and n_pages % PAGES_PER_STEP == 0
    npg = PAGES_PER_STEP
    nsteps = n_pages // npg

    def page_spec(i):
        return pl.BlockSpec((1, SB_W, page),
                            lambda bi, si, pt: (pt[bi * n_pages + (n_pages - 1 - (si * npg + i))], 0, 0))

    n = npg * SB_HEADS
    grid_spec = pltpu.PrefetchScalarGridSpec(
        num_scalar_prefetch=1,
        grid=(b, nsteps),
        in_specs=[pl.BlockSpec((1, 1, SB_W), lambda bi, si, pt: (bi, 0, 0)),
                  pl.BlockSpec((SB_HEADS, LANES), lambda bi, si, pt: (0, 0)),
                  pl.BlockSpec((2 * LANES, 2 * LANES), lambda bi, si, pt: (0, 0)),
                  pl.BlockSpec((2 * n, 2 * n), lambda bi, si, pt: (0, 0))]
                 + [page_spec(i) for i in range(npg)] * 2,
        out_specs=pl.BlockSpec((1, SB_W, 1), lambda bi, si, pt: (bi, 0, 0)),
        scratch_shapes=[pltpu.VMEM((SB_HEADS, LANES), F32), pltpu.VMEM((SB_W, LANES), F32)])
    bias_rep = jnp.broadcast_to(bias.astype(F32)[:, None], (SB_HEADS, LANES))
    out = pl.pallas_call(
        _sb_sample_kernel,
        grid_spec=grid_spec,
        out_shape=jax.ShapeDtypeStruct((b, SB_W, 1), F32),
        compiler_params=pltpu.CompilerParams(dimension_semantics=("arbitrary", "arbitrary"),
                                             vmem_limit_bytes=VMEM_LIMIT),
    )(page_table.reshape(-1), q_bf.reshape(b, 1, SB_W), bias_rep, _suffix_matrix(), _page_matrices(),
      *([ckt] * npg), *([cvt] * npg))
    return out.reshape(b, SB_W)


def _hgrn_sample_kernel(lbl_ref, s_ref, q_ref, f_ref, v_ref, o_ref, sn_ref):
    l0 = lbl_ref[0]
    l1 = lbl_ref[1]
    m = jnp.maximum(l0, l1)
    e0 = jnp.exp(l0 - m)
    e1 = jnp.exp(l1 - m)
    lb = e0 / (e0 + e1)
    f = lb + (1.0 - lb) * jax.nn.sigmoid(f_ref[0])
    k = 1.0 - f
    v = v_ref[0]
    v_rows = jnp.concatenate([jnp.broadcast_to(v[:, h * HG_D:(h + 1) * HG_D], (HG_D, HG_D))
                              for h in range(HG_HEADS)], axis=0)
    s_new = f * s_ref[0] + k * v_rows
    sn_ref[0] = s_new
    qs = q_ref[0] * s_new
    o_ref[0] = jnp.sum(qs.reshape(HG_HEADS, HG_D, HG_D), axis=1)


def _hgrn_sample(state, q_col, f_col, v_row, lb_logits):
    b = state.shape[0]
    row = lambda shape: pl.BlockSpec((1,) + shape, lambda bi: (bi, 0, 0))
    return pl.pallas_call(
        _hgrn_sample_kernel,
        grid=(b,),
        in_specs=[pl.BlockSpec((2, HG_W, 1), lambda bi: (0, 0, 0)),
                  row((HG_W, HG_D)), row((HG_W, 1)), row((HG_W, 1)), row((1, HG_W))],
        out_specs=[row((HG_HEADS, HG_D)), row((HG_W, HG_D))],
        out_shape=[jax.ShapeDtypeStruct((b, HG_HEADS, HG_D), F32), jax.ShapeDtypeStruct((b, HG_W, HG_D), F32)],
        compiler_params=pltpu.CompilerParams(dimension_semantics=("arbitrary",)),
    )(lb_logits.reshape(2, HG_W, 1), state, q_col, f_col, v_row)


def _in_proj_outs(transposed_kv):
    qs = (LOG2E if transposed_kv else 1.0) / math.sqrt(SB_DH)
    q_out = (0, SB_W, BF16, qs)
    rest_out = (3 * SB_W, 3 * SB_W + D_REST, F32, 1.0)
    if transposed_kv:
        return [q_out, rest_out]
    return [q_out, (SB_W, 2 * SB_W, F32, 1.0), (2 * SB_W, 3 * SB_W, F32, 1.0), rest_out]


def kernel(x_prompt, x_sample, mem_prompt, cache_k, cache_v, page_table, state_hgrn, cache_mem_k, cache_mem_v,
           norm_gain, w_in, sb_bias, hg_lb_logits, hg_norm_gain, mem_norm_gain, w_mem_kv, w_out, final_norm_gain):
    batch, seq, d = x_prompt.shape
    dec_b = x_sample.shape[0]
    n_mem = mem_prompt.shape[1]
    depth = w_in.shape[0]
    assert depth == 1 and x_sample.shape[1] == 1
    l = 0
    lb_logits = hg_lb_logits.astype(F32)
    assert lb_logits.shape[0] == 2
    w_in_bf = w_in[l].astype(BF16)
    w_out_bf = w_out[l].astype(BF16)
    w_mem_bf = w_mem_kv[l].astype(BF16)
    bias = sb_bias[l].astype(F32)

    w_kv_t = w_in[l][:, SB_W:3 * SB_W].T.astype(BF16)
    q_bf, rest, kt_p, vt_p = _norm_proj(x_prompt, norm_gain[l], w_in_bf, _in_proj_outs(True), tm=256,
                                        wt_bf=w_kv_t, t_widths=(SB_W, SB_W))
    mkt, mvt = _norm_proj(mem_prompt, mem_norm_gain[l], w_mem_bf, [], tm=256,
                          wt_bf=w_mem_kv[l].T.astype(BF16), t_widths=(XA_W, XA_W))
    qs_bf, k_s, v_s, rest_s = _norm_proj(x_sample.reshape(1, dec_b, d), norm_gain[l], w_in_bf,
                                         _in_proj_outs(False), tm=dec_b)

    n_pool, page = cache_k.shape[1], cache_k.shape[2]
    ckt = jnp.transpose(cache_k[l], (0, 2, 3, 1)).reshape(n_pool, SB_W, page)
    cvt = jnp.transpose(cache_v[l], (0, 2, 3, 1)).reshape(n_pool, SB_W, page)
    n_pages = page_table.shape[1]
    fused = (n_pages % PAGES_PER_STEP == 0 and
             dec_b * (n_pages // PAGES_PER_STEP) == batch * (SB_W // LANES) * (seq // SB_TQ))
    if fused:
        sb_o, sb_o_s = _sb_prompt(q_bf, kt_p, vt_p, bias, sample=(qs_bf, bias, page_table, ckt, cvt))
    else:
        sb_o, _ = _sb_prompt(q_bf, kt_p, vt_p, bias)
        sb_o_s = _sb_sample(qs_bf, bias, page_table, ckt, cvt)

    rest3 = rest.reshape(batch, seq, D_REST)
    hg_o, st_t = _hgrn_prompt(rest3, lb_logits)
    y_prompt = _mix_out(x_prompt, sb_o.reshape(batch, seq, SB_W), hg_o, rest3, mkt, mvt,
                        w_out_bf, hg_norm_gain[l], final_norm_gain, tm=512)
    hgrn_prompt = jnp.stack([st_t[:, h * HG_D:(h + 1) * HG_D, h * HG_D:(h + 1) * HG_D]
                             for h in range(HG_HEADS)], axis=1).swapaxes(-1, -2)

    hq_s = rest_s[:, SB_W:SB_W + HG_W].reshape(dec_b, HG_W, 1)
    hf_s = rest_s[:, SB_W + HG_W:SB_W + 2 * HG_W].reshape(dec_b, HG_W, 1)
    hv_s = rest_s[:, SB_W + 2 * HG_W:SB_W + 3 * HG_W].reshape(dec_b, 1, HG_W)
    hg_o_s, st_s = _hgrn_sample(state_hgrn[l].astype(F32).reshape(dec_b, HG_W, HG_D), hq_s, hf_s, hv_s, lb_logits)
    y_sample = _mix_out(x_sample, sb_o_s.reshape(dec_b, 1, SB_W), hg_o_s.reshape(dec_b, 1, HG_W),
                        rest_s.reshape(dec_b, 1, D_REST),
                        jnp.transpose(cache_mem_k[l], (0, 2, 3, 1)).reshape(dec_b, XA_W, n_mem),
                        jnp.transpose(cache_mem_v[l], (0, 2, 3, 1)).reshape(dec_b, XA_W, n_mem),
                        w_out_bf, hg_norm_gain[l], final_norm_gain, tm=1)

    def token_major(a_t, heads, dh):
        b_, _, t_ = a_t.shape
        return jnp.transpose(a_t.reshape(b_, heads, dh, t_), (0, 3, 1, 2))[None]

    return (y_prompt, y_sample, token_major(kt_p, SB_HEADS, SB_DH), token_major(vt_p, SB_HEADS, SB_DH),
            hgrn_prompt.reshape(1, batch, HG_HEADS, HG_D, HG_D),
            token_major(mkt, XA_HEADS, XA_DH), token_major(mvt, XA_HEADS, XA_DH),
            k_s.reshape(1, dec_b, 1, SB_HEADS, SB_DH), v_s.reshape(1, dec_b, 1, SB_HEADS, SB_DH),
            st_s.reshape(1, dec_b, HG_HEADS, HG_D, HG_D).astype(state_hgrn.dtype))
```

```python
import functools
import math

import numpy as np
import jax
import jax.numpy as jnp
from jax import lax
from jax.experimental import pallas as pl
from jax.experimental.pallas import tpu as pltpu

F32 = jnp.float32
BF16 = jnp.bfloat16
EPS = 1e-6

SB_HEADS, SB_DH = 8, 64
HG_HEADS, HG_D = 4, 64
XA_HEADS, XA_DH = 4, 64
SB_W = SB_HEADS * SB_DH
HG_W = HG_HEADS * HG_D
XA_W = XA_HEADS * XA_DH
D_REST = SB_W + 3 * HG_W + HG_W + 2 * XA_W

LANES = 128
SB_TQ = 512
SB_TK = 128
NEG_BIG = -1e30
LOG2E = math.log2(math.e)
HG_CHUNK = 128
HG_CHUNKS_PER_STEP = 4
PAGES_PER_STEP = 16
SAMPLE_ROWS_PER_STEP = 8
PAGES_PER_DOT = 4
VMEM_LIMIT = 48 * 1024 * 1024
SB_VMEM_LIMIT = 56 * 1024 * 1024

_NT = (((1,), (1,)), ((), ()))
_TN = (((0,), (0,)), ((), ()))


def _split_bf16(x):
    hi = x.astype(BF16)
    lo = (x - hi.astype(F32)).astype(BF16)
    return hi, lo


def _log_sigmoid_pair(t):
    sp = jnp.log1p(jnp.exp(-jnp.abs(t)))
    log_beta = jnp.minimum(t, 0.0) - sp
    return log_beta, log_beta - t


def _norm_proj_kernel(x_ref, g_ref, w_ref, *refs, outs, t_widths):
    wt_ref, out_refs = (refs[0], refs[1:]) if t_widths else (None, refs)
    x = x_ref[...]
    ms = jnp.mean(x * x, axis=-1, keepdims=True)
    xn = ((x * lax.rsqrt(ms + EPS)) * g_ref[...]).astype(BF16)
    cache = {}
    for o_ref, (lo, hi, scale) in zip(out_refs, outs):
        if (lo, hi) not in cache:
            cache[(lo, hi)] = jnp.dot(xn, w_ref[:, lo:hi], preferred_element_type=F32)
        h = cache[(lo, hi)]
        if scale != 1.0:
            h = h * scale
        o_ref[...] = h.astype(o_ref.dtype)
    off = 0
    for o_ref, width in zip(out_refs[len(outs):], t_widths):
        o_ref[0] = lax.dot_general(wt_ref[off:off + width, :], xn, _NT, preferred_element_type=F32)
        off += width


def _norm_proj(x3, gain, w_bf, outs, tm, wt_bf=None, t_widths=()):
    b, t, d = x3.shape
    m = b * t
    n = w_bf.shape[1]
    tm = min(tm, m)
    assert (t % tm == 0 or tm == m) and m % tm == 0
    nbt = max(t // tm, 1)
    assert (wt_bf is None) == (len(t_widths) == 0)
    t_args = [] if wt_bf is None else [wt_bf]
    kern = functools.partial(_norm_proj_kernel, outs=[(lo, hi, sc) for lo, hi, _, sc in outs], t_widths=tuple(t_widths))
    return pl.pallas_call(
        kern,
        grid=(m // tm,),
        in_specs=[pl.BlockSpec((tm, d), lambda i: (i, 0)),
                  pl.BlockSpec((1, d), lambda i: (0, 0)),
                  pl.BlockSpec((d, n), lambda i: (0, 0))]
                 + [pl.BlockSpec(a.shape, lambda i: (0, 0)) for a in t_args],
        out_specs=[pl.BlockSpec((tm, hi - lo), lambda i: (i, 0)) for lo, hi, _, _ in outs]
                  + [pl.BlockSpec((1, w, tm), lambda i: (i // nbt, 0, i % nbt)) for w in t_widths],
        out_shape=[jax.ShapeDtypeStruct((m, hi - lo), dt) for lo, hi, dt, _ in outs]
                  + [jax.ShapeDtypeStruct((b, w, t), F32) for w in t_widths],
        compiler_params=pltpu.CompilerParams(dimension_semantics=("arbitrary",), vmem_limit_bytes=VMEM_LIMIT),
    )(x3.reshape(m, d), gain.reshape(1, d), w_bf, *t_args)


def _suffix_matrix():
    jp = np.arange(LANES)[:, None]
    j = np.arange(LANES)[None, :]
    w = np.concatenate([(jp > j).astype(np.float32), np.ones((LANES, LANES), np.float32)], axis=1)
    return jnp.asarray(np.concatenate([w, w], axis=0), dtype=BF16)


def _sb_prompt_kernel(pt_ref, bias_ref, q_ref, kt_ref, vt_ref, w2_ref, *refs, npg, sample_steps):
    tq, tk = SB_TQ, SB_TK
    hp = pl.program_id(1)
    qi = pl.program_id(2)
    nkb = kt_ref.shape[2] // tk
    if npg:
        qs_ref, bias_s_ref, w2s_ref, pm_ref = refs[:4]
        page_refs = refs[4:4 + 2 * npg]
        o_ref, os_ref, kbd_ref, vbd_ref, carry_ref, acc_ref, lb_scr, x_scr, carry_s_ref, acc_s_ref = refs[4 + 2 * npg:]
        lin = (pl.program_id(0) * pl.num_programs(1) + hp) * pl.num_programs(2) + qi

        @pl.when(lin == 0)
        def _():
            carry_s_ref[...] = jnp.zeros_like(carry_s_ref)
            acc_s_ref[...] = jnp.zeros_like(acc_s_ref)

        sample = _sb_sample_parts(lin % sample_steps == 0, qs_ref, bias_s_ref, w2s_ref, pm_ref,
                                  page_refs[:npg], page_refs[npg:], os_ref, carry_s_ref, acc_s_ref)
    else:
        o_ref, kbd_ref, vbd_ref, carry_ref, acc_ref, lb_scr, x_scr = refs
        sample = None

    @pl.when(qi == 0)
    def _():
        row_v = lax.broadcasted_iota(jnp.int32, (LANES, tk), 0)
        for j in range(nkb):
            for src, dst in ((kt_ref, kbd_ref), (vt_ref, vbd_ref)):
                blk = src[0, :, j * tk:(j + 1) * tk].astype(BF16)
                zero = jnp.zeros_like(blk)
                dst[j, :, 0:tk] = jnp.where(row_v < SB_DH, blk, zero)
                dst[j, :, tk:2 * tk] = jnp.where(row_v >= SB_DH, blk, zero)

    q2 = q_ref[...]
    b2 = (bias_ref[2 * hp] * LOG2E, bias_ref[2 * hp + 1] * LOG2E)
    qpos = qi * tq + lax.broadcasted_iota(jnp.int32, (tq, LANES), 0)
    col = lax.broadcasted_iota(jnp.int32, (tq, LANES), 1)

    grp = tq // tk
    pend = 2

    def block_of(m, g):
        return (qi - m) * grp + (grp - 1 - g)

    def reset():
        carry_ref[...] = jnp.zeros_like(carry_ref)
        acc_ref[...] = jnp.zeros_like(acc_ref)

    def emit(tile):
        o_ref[pl.ds(pl.multiple_of(tile * tq, tq), tq), :] = acc_ref[...]

    def finish_scores(z, j, slot, g, masked):
        causal = (j * tk + col) < qpos if masked else None
        for half in range(2):
            rows = slice(half * tq, (half + 1) * tq)
            t = z[:, half * tk:(half + 1) * tk] + b2[half]
            sp = jnp.log(1.0 + jnp.exp2(-jnp.abs(t))) * LOG2E
            lb = jnp.minimum(t, 0.0) - sp
            l1m = lb - t
            if masked:
                l1m = jnp.where(causal, l1m, 0.0)
                lb = jnp.where(causal, lb, NEG_BIG)
            lb_scr[slot, g, rows, :] = lb
            x_scr[slot, g, rows, :] = l1m.astype(BF16)

    def finish_weights(r, j, slot, g):
        suffix = r[:, :LANES] + carry_ref[...]
        carry_ref[...] += r[:, LANES:]
        a = jnp.exp2(lb_scr[slot, g] + suffix).astype(BF16)
        a2 = jnp.concatenate([a[:tq], a[tq:]], axis=1)
        acc_ref[...] += lax.dot_general(a2, vbd_ref[j], _NT, preferred_element_type=F32)

    def stage(w_blocks, slot_w, m_s, slot_s, masked=False, between=None, rider=None):
        rs, zs = [], []
        for g in range(grp):
            if w_blocks is not None:
                rs.append(jnp.dot(x_scr[slot_w, g], w2_ref[...], preferred_element_type=F32))
            if m_s is not None:
                zs.append(jnp.dot(q2, kbd_ref[block_of(m_s, g)], preferred_element_type=F32))
        if rider is not None:
            n_pieces, r_scores, r_weights, r_begin, r_values, r_end = rider
            share = lambda g: range(g * n_pieces // grp, (g + 1) * n_pieces // grp)
            rz = []
        for g in range(grp):
            if w_blocks is not None:
                finish_weights(rs[g], w_blocks[g], slot_w, g)
            if rider is not None:
                for j in share(g):
                    rz += r_scores(j)
        if between is not None:
            between()
        if rider is not None:
            ra = r_weights(rz)
            racc = r_begin()
        for g in range(grp):
            if m_s is not None:
                finish_scores(zs[g], block_of(m_s, g), slot_s, g, masked)
            if rider is not None:
                for j in share(g):
                    racc = r_values(j, ra, racc)
        if rider is not None:
            r_end(racc)

    def blocks(m):
        return [block_of(m, g) for g in range(grp)]

    last_blocks = [grp - 1 - g for g in range(grp)]

    @pl.when(qi == 0)
    def _():
        reset()
        stage(None, None, 0, pend, masked=True, rider=sample)

    @pl.when(qi > 0)
    def _():
        def between():
            emit(qi - 1)
            reset()
        stage(last_blocks, pend, 0, 0, masked=True, between=between, rider=sample)

    def loop_body(p, c):
        stage(blocks(2 * p), 0, 2 * p + 1, 1)
        stage(blocks(2 * p + 1), 1, 2 * p + 2, 0)
        return c

    lax.fori_loop(0, (qi - 1) // 2, loop_body, 0)

    @pl.when(qi % 2 == 1)
    def _():
        stage(blocks(qi - 1), 0, qi, pend)

    @pl.when((qi % 2 == 0) & (qi > 0))
    def _():
        stage(blocks(qi - 2), 0, qi - 1, 1)
        stage(blocks(qi - 1), 1, qi, pend)

    @pl.when(qi == pl.num_programs(2) - 1)
    def _():
        stage(last_blocks, pend, None, None)
        emit(qi)


def _sb_prompt(q_bf, kt, vt, bias, sample=None):
    tq, tk = SB_TQ, SB_TK
    batch, _, seq = kt.shape
    assert seq % tq == 0 and tq % tk == 0
    nq = seq // tq
    npairs = SB_W // LANES
    const = lambda shape: pl.BlockSpec(shape, lambda b, h, i, pt: (0,) * len(shape))
    in_specs = [pl.BlockSpec(memory_space=pltpu.SMEM),
                pl.BlockSpec((tq, LANES), lambda b, h, i, pt: (b * nq + i, h)),
                pl.BlockSpec((1, LANES, seq), lambda b, h, i, pt: (b, h, 0)),
                pl.BlockSpec((1, LANES, seq), lambda b, h, i, pt: (b, h, 0)),
                const((LANES, 2 * LANES))]
    out_specs = [pl.BlockSpec((seq, LANES), lambda b, h, i, pt: (b, h))]
    out_shape = [jax.ShapeDtypeStruct((batch * seq, SB_W), F32)]
    scratch = [pltpu.VMEM((seq // tk, LANES, 2 * tk), BF16),
               pltpu.VMEM((seq // tk, LANES, 2 * tk), BF16),
               pltpu.VMEM((2 * tq, LANES), F32),
               pltpu.VMEM((tq, LANES), F32),
               pltpu.VMEM((3, tq // tk, 2 * tq, LANES), F32),
               pltpu.VMEM((3, tq // tk, 2 * tq, LANES), BF16)]
    args = [bias, q_bf, kt, vt, _suffix_matrix()[:LANES]]
    npg, sample_steps = 0, 1
    page_table = jnp.zeros((1,), jnp.int32)
    if sample is not None:
        qs_bf, bias_s, page_table, ckt, cvt = sample
        dec_b, n_pages = page_table.shape
        npg = PAGES_PER_STEP
        sample_steps = n_pages // npg
        page = ckt.shape[2]
        assert page == LANES and n_pages % npg == 0 and dec_b * sample_steps == batch * npairs * nq
        n = npg * SB_HEADS
        lin = lambda b, h, i: (b * npairs + h) * nq + i

        def page_spec(j):
            def index(b, h, i, pt):
                row, s = lin(b, h, i) // sample_steps, lin(b, h, i) % sample_steps
                return (pt[row * n_pages + (n_pages - 1 - (s * npg + j))], 0, 0)
            return pl.BlockSpec((1, SB_W, page), index)

        row_spec = pl.BlockSpec((1, 1, SB_W), lambda b, h, i, pt: (lin(b, h, i) // sample_steps, 0, 0))
        in_specs += [row_spec, const((SB_HEADS, LANES)), const((2 * LANES, 2 * LANES)), const((2 * n, 2 * n))]
        in_specs += [page_spec(j) for j in range(npg)] * 2
        out_specs.append(pl.BlockSpec((1, SB_W, 1), lambda b, h, i, pt: (lin(b, h, i) // sample_steps, 0, 0)))
        out_shape.append(jax.ShapeDtypeStruct((dec_b, SB_W, 1), F32))
        scratch += [pltpu.VMEM((SB_HEADS, LANES), F32), pltpu.VMEM((SB_W, LANES), F32)]
        args += [qs_bf.reshape(dec_b, 1, SB_W), jnp.broadcast_to(bias_s.astype(F32)[:, None], (SB_HEADS, LANES)),
                 _suffix_matrix(), _page_matrices()] + [ckt] * npg + [cvt] * npg
        page_table = page_table.reshape(-1)
    outs = pl.pallas_call(
        functools.partial(_sb_prompt_kernel, npg=npg, sample_steps=sample_steps),
        grid_spec=pltpu.PrefetchScalarGridSpec(num_scalar_prefetch=1, grid=(batch, npairs, nq), in_specs=in_specs,
                                               out_specs=out_specs, scratch_shapes=scratch),
        out_shape=out_shape,
        compiler_params=pltpu.CompilerParams(dimension_semantics=("arbitrary", "arbitrary", "arbitrary"),
                                             vmem_limit_bytes=SB_VMEM_LIMIT),
    )(page_table, *args)
    if sample is None:
        return outs[0], None
    return outs[0], outs[1].reshape(-1, SB_W)


def _hgrn_levels(c):
    return [c >> (i + 1) for i in range(int(math.log2(c)))]


def _hgrn_constants(c):
    t = np.arange(c)[:, None]
    u = np.arange(c)[None, :]
    rows = [(u <= t), (u > t)]
    masks = []
    for hs in _hgrn_levels(c):
        mid = (t // (2 * hs)) * (2 * hs) + hs - 1
        upper = (t % (2 * hs)) >= hs
        rows.append((upper & (u > mid) & (u <= t)) | ((~upper) & (u > t) & (u <= mid)))
        s = u
        masks.append(((t // (2 * hs)) == (s // (2 * hs))) & upper & ((s % (2 * hs)) < hs))
    masks.append(t == u)
    mall = np.concatenate(rows, axis=0).astype(np.float32)
    mall2 = np.concatenate([mall, mall], axis=1)
    return jnp.asarray(mall2, dtype=BF16), jnp.asarray(np.stack(masks).astype(np.float32))


def _forget_lower_bound(lbl_ref):
    l0 = lbl_ref[0:1]
    l1 = lbl_ref[1:2]
    m = jnp.maximum(l0, l1)
    e0 = jnp.exp(l0 - m)
    e1 = jnp.exp(l1 - m)
    return e0 / (e0 + e1)


def _hgrn_prompt_kernel(lbl_ref, hq_ref, hf_ref, hv_ref, mall_ref, bm_ref, o_ref, st_ref, s_scr, d_scr):
    c = HG_CHUNK
    ci = pl.program_id(1)
    levels = _hgrn_levels(c)
    n_sub = hq_ref.shape[1] // c

    @pl.when(ci == 0)
    def _():
        s_scr[...] = jnp.zeros_like(s_scr)

    lb = _forget_lower_bound(lbl_ref)
    lane = lax.broadcasted_iota(jnp.int32, (c, HG_W), 1)
    head_masks = [(lane >= h * HG_D) & (lane < (h + 1) * HG_D) for h in range(HG_HEADS)]
    r = lax.broadcasted_iota(jnp.int32, (HG_W, HG_W), 0)
    cc = lax.broadcasted_iota(jnp.int32, (HG_W, HG_W), 1)
    same_head = (r // HG_D) == (cc // HG_D)

    def by_head(x_bf):
        z = jnp.zeros_like(x_bf)
        return jnp.concatenate([jnp.where(m, x_bf, z) for m in head_masks], axis=0)

    def pair_scores(q_f32, k_f32, mask):
        p = lax.dot_general(q_f32.astype(BF16), by_head(k_f32.astype(BF16)), _NT, preferred_element_type=F32)
        return p * jnp.concatenate([mask] * HG_HEADS, axis=1)

    st = s_scr[...]
    for sub in range(n_sub):
        rows = slice(sub * c, (sub + 1) * c)
        d = d_scr.at[sub]
        q = hq_ref[0, rows, :]
        f = lb + (1.0 - lb) * jax.nn.sigmoid(hf_ref[0, rows, :])
        logf = jnp.log(f)
        k = 1.0 - f
        v_bf = hv_ref[0, rows, :].astype(BF16)
        hi, lo = _split_bf16(logf)
        d[...] = jnp.dot(mall_ref[...], jnp.concatenate([hi, lo], axis=0), preferred_element_type=F32)

        attn = pair_scores(q, k, bm_ref[len(levels)])
        for li in range(len(levels)):
            e = jnp.exp(d[(2 + li) * c:(3 + li) * c, :])
            attn = attn + pair_scores(q * e, k * e, bm_ref[li])
        o = jnp.dot(attn.astype(BF16), by_head(v_bf), preferred_element_type=F32)

        bcum = d[0:c, :]
        o = o + lax.dot_general((q * jnp.exp(bcum)).astype(BF16), st.astype(BF16), _NT,
                                preferred_element_type=F32)
        o_ref[0, rows, :] = o

        k_dec = (k * jnp.exp(d[c:2 * c, :])).astype(BF16)
        upd = lax.dot_general(v_bf, k_dec, _TN, preferred_element_type=F32)
        st = st * jnp.exp(bcum[c - 1:c, :]) + jnp.where(same_head, upd, 0.0)
    s_scr[...] = st

    @pl.when(ci == pl.num_programs(1) - 1)
    def _():
        st_ref[0] = st


def _hgrn_prompt(rest3, lb_logits):
    b, t, _ = rest3.shape
    c = HG_CHUNK
    mall2, bm = _hgrn_constants(c)
    nr = mall2.shape[0]
    n_sub = HG_CHUNKS_PER_STEP if t % (HG_CHUNKS_PER_STEP * c) == 0 else 1
    rows = n_sub * c
    col = lambda j: (lambda bi, ci: (bi, ci, j))
    return pl.pallas_call(
        _hgrn_prompt_kernel,
        grid=(b, t // rows),
        in_specs=[pl.BlockSpec((2, HG_W), lambda bi, ci: (0, 0)),
                  pl.BlockSpec((1, rows, HG_W), col(2)),
                  pl.BlockSpec((1, rows, HG_W), col(3)),
                  pl.BlockSpec((1, rows, HG_W), col(4)),
                  pl.BlockSpec((nr, 2 * c), lambda bi, ci: (0, 0)),
                  pl.BlockSpec(bm.shape, lambda bi, ci: (0, 0, 0))],
        out_specs=[pl.BlockSpec((1, rows, HG_W), lambda bi, ci: (bi, ci, 0)),
                   pl.BlockSpec((1, HG_W, HG_W), lambda bi, ci: (bi, 0, 0))],
        out_shape=[jax.ShapeDtypeStruct((b, t, HG_W), F32), jax.ShapeDtypeStruct((b, HG_W, HG_W), F32)],
        scratch_shapes=[pltpu.VMEM((HG_W, HG_W), F32), pltpu.VMEM((n_sub, nr, HG_W), F32)],
        compiler_params=pltpu.CompilerParams(dimension_semantics=("arbitrary", "arbitrary"),
                                             vmem_limit_bytes=VMEM_LIMIT),
    )(lb_logits, rest3, rest3, rest3, mall2, bm)


def _silu(g):
    return g * jax.nn.sigmoid(g)


def _mix_out_kernel(x_ref, sbo_ref, hgo_ref, sbg_ref, hgg_ref, xaq_ref, xag_ref, mkt_ref, mvt_ref,
                    wout_ref, hgn_ref, bd2_ref, fng_ref, y_ref):
    nb, tm = x_ref.shape[0], x_ref.shape[1]
    assert nb == 1 or tm == 1
    pad = nb == 1 and tm < 8

    def load(ref):
        if nb > 1:
            return ref[:, 0, :]
        a = ref[0]
        return jnp.broadcast_to(a, (8, a.shape[-1])) if pad else a

    x = load(x_ref)
    n_mem = mkt_ref.shape[2]
    row_m = lax.broadcasted_iota(jnp.int32, (XA_W, n_mem), 0)

    def cross_attend(xq_f32, mkt, mvt):
        r = xq_f32.shape[0]
        lane = lax.broadcasted_iota(jnp.int32, (r, XA_W), 1)
        xq = (xq_f32 * (1.0 / math.sqrt(XA_DH))).astype(BF16)
        zq = jnp.zeros_like(xq)
        q_st = jnp.concatenate([jnp.where((lane >= h * XA_DH) & (lane < (h + 1) * XA_DH), xq, zq)
                                for h in range(XA_HEADS)], axis=0)
        s = jnp.dot(q_st, mkt.astype(BF16), preferred_element_type=F32)
        p = jnp.exp(s - jnp.max(s, axis=-1, keepdims=True))
        p = (p / jnp.sum(p, axis=-1, keepdims=True)).astype(BF16)
        p2 = jnp.concatenate([p[h * r:(h + 1) * r] for h in range(XA_HEADS)], axis=1)
        mvt_bf = mvt.astype(BF16)
        zv = jnp.zeros_like(mvt_bf)
        mvt_st = jnp.concatenate([jnp.where((row_m >= h * XA_DH) & (row_m < (h + 1) * XA_DH), mvt_bf, zv)
                                  for h in range(XA_HEADS)], axis=1)
        return lax.dot_general(p2, mvt_st, _NT, preferred_element_type=F32)

    xaq = load(xaq_ref)
    if nb == 1:
        xa_o = cross_attend(xaq, mkt_ref[0], mvt_ref[0])
    else:
        xa_o = jnp.concatenate([cross_attend(jnp.broadcast_to(xaq[i:i + 1], (8, XA_W)), mkt_ref[i], mvt_ref[i])[0:1]
                                for i in range(nb)], axis=0)

    hg = load(hgo_ref)
    hi, lo = _split_bf16(hg * hg)
    ms = jnp.dot(jnp.concatenate([hi, lo], axis=1), bd2_ref[...], preferred_element_type=F32) * (1.0 / HG_D)
    hg_n = (hg * lax.rsqrt(ms + EPS)) * hgn_ref[...]

    sb = (load(sbo_ref) * _silu(load(sbg_ref))).astype(BF16)
    hgm = (hg_n * _silu(load(hgg_ref))).astype(BF16)
    xam = (xa_o * _silu(load(xag_ref))).astype(BF16)
    y = x + jnp.dot(sb, wout_ref[0:SB_W, :], preferred_element_type=F32)
    y = y + jnp.dot(hgm, wout_ref[SB_W:SB_W + HG_W, :], preferred_element_type=F32)
    y = y + jnp.dot(xam, wout_ref[SB_W + HG_W:, :], preferred_element_type=F32)
    msy = jnp.mean(y * y, axis=-1, keepdims=True)
    y = (y * lax.rsqrt(msy + EPS)) * fng_ref[...]
    if nb == 1:
        y_ref[0] = y[0:tm]
    else:
        y_ref[:, 0, :] = y


def _mix_out(x3, sbo3, hgo3, rest3, mkt3, mvt3, wout_bf, hg_norm_gain, final_gain, tm):
    b, t, d = x3.shape
    nm = mkt3.shape[2]
    nb = SAMPLE_ROWS_PER_STEP if (t == 1 and b % SAMPLE_ROWS_PER_STEP == 0) else 1
    assert t % tm == 0 and (nb == 1 or tm == 1)
    bd = (np.arange(HG_W)[:, None] // HG_D) == (np.arange(HG_W)[None, :] // HG_D)
    bd2 = jnp.asarray(np.concatenate([bd, bd], axis=0).astype(np.float32), dtype=BF16)
    blk = lambda w, j: pl.BlockSpec((nb, tm, w), lambda bi, ti: (bi, ti, j))
    const = lambda shape: pl.BlockSpec(shape, lambda bi, ti: (0,) * len(shape))
    return pl.pallas_call(
        _mix_out_kernel,
        grid=(b // nb, t // tm),
        in_specs=[blk(d, 0), blk(SB_W, 0), blk(HG_W, 0),
                  blk(SB_W, 0),
                  blk(HG_W, 5),
                  blk(XA_W, 6),
                  blk(XA_W, 7),
                  pl.BlockSpec((nb, XA_W, nm), lambda bi, ti: (bi, 0, 0)),
                  pl.BlockSpec((nb, XA_W, nm), lambda bi, ti: (bi, 0, 0)),
                  const(wout_bf.shape), const((1, HG_W)), const(bd2.shape), const((1, d))],
        out_specs=blk(d, 0),
        out_shape=jax.ShapeDtypeStruct((b, t, d), F32),
        compiler_params=pltpu.CompilerParams(dimension_semantics=("arbitrary", "arbitrary"),
                                             vmem_limit_bytes=VMEM_LIMIT),
    )(x3, sbo3, hgo3, rest3, rest3, rest3, rest3, mkt3, mvt3, wout_bf,
      hg_norm_gain.reshape(1, HG_W), bd2, final_gain.reshape(1, d))


def _page_matrices():
    n = PAGES_PER_STEP * SB_HEADS
    r = np.arange(n)[:, None]
    c = np.arange(n)[None, :]
    same_head = (r % SB_HEADS) == (c % SB_HEADS)
    before = same_head & ((c // SB_HEADS) < (r // SB_HEADS))
    mats = [np.concatenate([m, m], axis=1).astype(np.float32) for m in (before, same_head)]
    return jnp.asarray(np.concatenate(mats, axis=0), dtype=BF16)


def _sb_sample_parts(first, q_ref, bias_ref, w2_ref, pm_ref, kt_refs, vt_refs, o_ref, carry_ref, acc_ref):
    npg = len(kt_refs)
    n = npg * SB_HEADS
    row8 = lax.broadcasted_iota(jnp.int32, (SB_HEADS, SB_W), 0)
    lane8 = lax.broadcasted_iota(jnp.int32, (SB_HEADS, SB_W), 1)
    own = (lane8 // SB_DH) == row8

    grp = PAGES_PER_DOT
    assert npg % grp == 0

    def pages(refs, i0):
        return jnp.concatenate([refs[i][0] for i in range(i0, i0 + grp)], axis=1).astype(BF16)

    n_pieces = npg // grp

    def scores_piece(j):
        qb = jnp.broadcast_to(q_ref[0].astype(F32), (SB_HEADS, SB_W))
        q_bd = jnp.where(own, qb, 0.0).astype(BF16)
        zc = jnp.dot(q_bd, pages(kt_refs, j * grp), preferred_element_type=F32)
        return [zc[:, i * LANES:(i + 1) * LANES] for i in range(grp)]

    def weights(z_pages):
        z = jnp.concatenate(z_pages, axis=0)
        bias_t = jnp.concatenate([bias_ref[...]] * npg, axis=0)
        log_beta, log_1m = _log_sigmoid_pair(z + bias_t)
        hi, lo = _split_bf16(log_1m)
        r = jnp.dot(jnp.concatenate([hi, lo], axis=1), w2_ref[...], preferred_element_type=F32)
        tot = r[:, LANES:]
        thi, tlo = _split_bf16(tot)
        cross = jnp.dot(pm_ref[...], jnp.concatenate([thi, tlo], axis=0), preferred_element_type=F32)
        carry = jnp.where(first, 0.0, carry_ref[...])
        suffix = r[:, :LANES] + cross[:n] + jnp.concatenate([carry] * npg, axis=0)
        carry_ref[...] = carry + cross[n:n + SB_HEADS]
        return jnp.exp(log_beta + suffix)

    def values_begin():
        return jnp.where(first, 0.0, acc_ref[...])

    def values_piece(j, a, acc):
        for i in range(j * grp, (j + 1) * grp):
            a_rows = jnp.concatenate([jnp.broadcast_to(a[i * SB_HEADS + h:i * SB_HEADS + h + 1, :], (SB_DH, LANES))
                                      for h in range(SB_HEADS)], axis=0)
            acc = acc + vt_refs[i][0] * a_rows
        return acc

    def values_end(acc):
        acc_ref[...] = acc
        o_ref[0] = jnp.sum(acc, axis=1, keepdims=True)

    return n_pieces, scores_piece, weights, values_begin, values_piece, values_end


def _run_sample_parts(parts):
    n_pieces, scores_piece, weights, values_begin, values_piece, values_end = parts
    z = []
    for j in range(n_pieces):
        z += scores_piece(j)
    a = weights(z)
    acc = values_begin()
    for j in range(n_pieces):
        acc = values_piece(j, a, acc)
    values_end(acc)


def _sb_sample_kernel(pt_ref, q_ref, bias_ref, w2_ref, pm_ref, *refs):
    npg = PAGES_PER_STEP
    o_ref, carry_ref, acc_ref = refs[2 * npg:]

    @pl.when((pl.program_id(0) == 0) & (pl.program_id(1) == 0))
    def _():
        carry_ref[...] = jnp.zeros_like(carry_ref)
        acc_ref[...] = jnp.zeros_like(acc_ref)

    _run_sample_parts(_sb_sample_parts(pl.program_id(1) == 0, q_ref, bias_ref, w2_ref, pm_ref,
                                       refs[:npg], refs[npg:2 * npg], o_ref, carry_ref, acc_ref))


def _sb_sample(q_bf, bias, page_table, ckt, cvt):
    b = q_bf.shape[0]
    n_pages = page_table.shape[1]
    page = ckt.shape[2]
    assert page == LANES and n_pages % PAGES_PER_STEP == 0
    npg = PAGES_PER_STEP
    nsteps = n_pages // npg

    def page_spec(i):
        return pl.BlockSpec((1, SB_W, page),
                            lambda bi, si, pt: (pt[bi * n_pages + (n_pages - 1 - (si * npg + i))], 0, 0))

    n = npg * SB_HEADS
    grid_spec = pltpu.PrefetchScalarGridSpec(
        num_scalar_prefetch=1,
        grid=(b, nsteps),
        in_specs=[pl.BlockSpec((1, 1, SB_W), lambda bi, si, pt: (bi, 0, 0)),
                  pl.BlockSpec((SB_HEADS, LANES), lambda bi, si, pt: (0, 0)),
                  pl.BlockSpec((2 * LANES, 2 * LANES), lambda bi, si, pt: (0, 0)),
                  pl.BlockSpec((2 * n, 2 * n), lambda bi, si, pt: (0, 0))]
                 + [page_spec(i) for i in range(npg)] * 2,
        out_specs=pl.BlockSpec((1, SB_W, 1), lambda bi, si, pt: (bi, 0, 0)),
        scratch_shapes=[pltpu.VMEM((SB_HEADS, LANES), F32), pltpu.VMEM((SB_W, LANES), F32)])
    bias_rep = jnp.broadcast_to(bias.astype(F32)[:, None], (SB_HEADS, LANES))
    out = pl.pallas_call(
        _sb_sample_kernel,
        grid_spec=grid_spec,
        out_shape=jax.ShapeDtypeStruct((b, SB_W, 1), F32),
        compiler_params=pltpu.CompilerParams(dimension_semantics=("arbitrary", "arbitrary"),
                                             vmem_limit_bytes=VMEM_LIMIT),
    )(page_table.reshape(-1), q_bf.reshape(b, 1, SB_W), bias_rep, _suffix_matrix(), _page_matrices(),
      *([ckt] * npg), *([cvt] * npg))
    return out.reshape(b, SB_W)


def _hgrn_sample_kernel(lbl_ref, s_ref, q_ref, f_ref, v_ref, o_ref, sn_ref):
    l0 = lbl_ref[0]
    l1 = lbl_ref[1]
    m = jnp.maximum(l0, l1)
    e0 = jnp.exp(l0 - m)
    e1 = jnp.exp(l1 - m)
    lb = e0 / (e0 + e1)
    for r in range(s_ref.shape[0]):
        f = lb + (1.0 - lb) * jax.nn.sigmoid(f_ref[r])
        k = 1.0 - f
        v = v_ref[r]
        v_rows = jnp.concatenate([jnp.broadcast_to(v[:, h * HG_D:(h + 1) * HG_D], (HG_D, HG_D))
                                  for h in range(HG_HEADS)], axis=0)
        s_new = f * s_ref[r] + k * v_rows
        sn_ref[r] = s_new
        qs = q_ref[r] * s_new
        o_ref[r] = jnp.sum(qs.reshape(HG_HEADS, HG_D, HG_D), axis=1)


def _hgrn_sample(state, q_col, f_col, v_row, lb_logits):
    b = state.shape[0]
    rows = SAMPLE_ROWS_PER_STEP if b % SAMPLE_ROWS_PER_STEP == 0 else 1
    row = lambda shape: pl.BlockSpec((rows,) + shape, lambda bi: (bi, 0, 0))
    return pl.pallas_call(
        _hgrn_sample_kernel,
        grid=(b // rows,),
        in_specs=[pl.BlockSpec((2, HG_W, 1), lambda bi: (0, 0, 0)),
                  row((HG_W, HG_D)), row((HG_W, 1)), row((HG_W, 1)), row((1, HG_W))],
        out_specs=[row((HG_HEADS, HG_D)), row((HG_W, HG_D))],
        out_shape=[jax.ShapeDtypeStruct((b, HG_HEADS, HG_D), F32), jax.ShapeDtypeStruct((b, HG_W, HG_D), F32)],
        compiler_params=pltpu.CompilerParams(dimension_semantics=("arbitrary",)),
    )(lb_logits.reshape(2, HG_W, 1), state, q_col, f_col, v_row)


def _in_proj_outs(transposed_kv):
    qs = (LOG2E if transposed_kv else 1.0) / math.sqrt(SB_DH)
    q_out = (0, SB_W, BF16, qs)
    rest_out = (3 * SB_W, 3 * SB_W + D_REST, F32, 1.0)
    if transposed_kv:
        return [q_out, rest_out]
    return [q_out, (SB_W, 2 * SB_W, F32, 1.0), (2 * SB_W, 3 * SB_W, F32, 1.0), rest_out]


def kernel(x_prompt, x_sample, mem_prompt, cache_k, cache_v, page_table, state_hgrn, cache_mem_k, cache_mem_v,
           norm_gain, w_in, sb_bias, hg_lb_logits, hg_norm_gain, mem_norm_gain, w_mem_kv, w_out, final_norm_gain):
    batch, seq, d = x_prompt.shape
    dec_b = x_sample.shape[0]
    n_mem = mem_prompt.shape[1]
    depth = w_in.shape[0]
    assert depth == 1 and x_sample.shape[1] == 1
    l = 0
    lb_logits = hg_lb_logits.astype(F32)
    assert lb_logits.shape[0] == 2
    w_in_bf = w_in[l].astype(BF16)
    w_out_bf = w_out[l].astype(BF16)
    w_mem_bf = w_mem_kv[l].astype(BF16)
    bias = sb_bias[l].astype(F32)

    w_kv_t = w_in[l][:, SB_W:3 * SB_W].T.astype(BF16)
    q_bf, rest, kt_p, vt_p = _norm_proj(x_prompt, norm_gain[l], w_in_bf, _in_proj_outs(True), tm=256,
                                        wt_bf=w_kv_t, t_widths=(SB_W, SB_W))
    mkt, mvt = _norm_proj(mem_prompt, mem_norm_gain[l], w_mem_bf, [], tm=256,
                          wt_bf=w_mem_kv[l].T.astype(BF16), t_widths=(XA_W, XA_W))
    qs_bf, k_s, v_s, rest_s = _norm_proj(x_sample.reshape(1, dec_b, d), norm_gain[l], w_in_bf,
                                         _in_proj_outs(False), tm=dec_b)

    n_pool, page = cache_k.shape[1], cache_k.shape[2]
    ckt = jnp.transpose(cache_k[l], (0, 2, 3, 1)).reshape(n_pool, SB_W, page)
    cvt = jnp.transpose(cache_v[l], (0, 2, 3, 1)).reshape(n_pool, SB_W, page)
    n_pages = page_table.shape[1]
    fused = (n_pages % PAGES_PER_STEP == 0 and
             dec_b * (n_pages // PAGES_PER_STEP) == batch * (SB_W // LANES) * (seq // SB_TQ))
    if fused:
        sb_o, sb_o_s = _sb_prompt(q_bf, kt_p, vt_p, bias, sample=(qs_bf, bias, page_table, ckt, cvt))
    else:
        sb_o, _ = _sb_prompt(q_bf, kt_p, vt_p, bias)
        sb_o_s = _sb_sample(qs_bf, bias, page_table, ckt, cvt)

    rest3 = rest.reshape(batch, seq, D_REST)
    hg_o, st_t = _hgrn_prompt(rest3, lb_logits)
    y_prompt = _mix_out(x_prompt, sb_o.reshape(batch, seq, SB_W), hg_o, rest3, mkt, mvt,
                        w_out_bf, hg_norm_gain[l], final_norm_gain, tm=512)
    hgrn_prompt = jnp.stack([st_t[:, h * HG_D:(h + 1) * HG_D, h * HG_D:(h + 1) * HG_D]
                             for h in range(HG_HEADS)], axis=1).swapaxes(-1, -2)

    hq_s = rest_s[:, SB_W:SB_W + HG_W].reshape(dec_b, HG_W, 1)
    hf_s = rest_s[:, SB_W + HG_W:SB_W + 2 * HG_W].reshape(dec_b, HG_W, 1)
    hv_s = rest_s[:, SB_W + 2 * HG_W:SB_W + 3 * HG_W].reshape(dec_b, 1, HG_W)
    hg_o_s, st_s = _hgrn_sample(state_hgrn[l].astype(F32).reshape(dec_b, HG_W, HG_D), hq_s, hf_s, hv_s, lb_logits)
    y_sample = _mix_out(x_sample, sb_o_s.reshape(dec_b, 1, SB_W), hg_o_s.reshape(dec_b, 1, HG_W),
                        rest_s.reshape(dec_b, 1, D_REST),
                        jnp.transpose(cache_mem_k[l], (0, 2, 3, 1)).reshape(dec_b, XA_W, n_mem),
                        jnp.transpose(cache_mem_v[l], (0, 2, 3, 1)).reshape(dec_b, XA_W, n_mem),
                        w_out_bf, hg_norm_gain[l], final_norm_gain, tm=1)

    def token_major(a_t, heads, dh):
        b_, _, t_ = a_t.shape
        return jnp.transpose(a_t.reshape(b_, heads, dh, t_), (0, 3, 1, 2))[None]

    return (y_prompt, y_sample, token_major(kt_p, SB_HEADS, SB_DH), token_major(vt_p, SB_HEADS, SB_DH),
            hgrn_prompt.reshape(1, batch, HG_HEADS, HG_D, HG_D),
            token_major(mkt, XA_HEADS, XA_DH), token_major(mvt, XA_HEADS, XA_DH),
            k_s.reshape(1, dec_b, 1, SB_HEADS, SB_DH), v_s.reshape(1, dec_b, 1, SB_HEADS, SB_DH),
            st_s.reshape(1, dec_b, HG_HEADS, HG_D, HG_D).astype(state_hgrn.dtype))
```

```python
import functools
import math

import numpy as np
import jax
import jax.numpy as jnp
from jax import lax
from jax.experimental import pallas as pl
from jax.experimental.pallas import tpu as pltpu

F32 = jnp.float32
BF16 = jnp.bfloat16
EPS = 1e-6

SB_HEADS, SB_DH = 8, 64
HG_HEADS, HG_D = 4, 64
XA_HEADS, XA_DH = 4, 64
SB_W = SB_HEADS * SB_DH
HG_W = HG_HEADS * HG_D
XA_W = XA_HEADS * XA_DH
D_REST = SB_W + 3 * HG_W + HG_W + 2 * XA_W

LANES = 128
SB_TQ = 512
SB_TK = 128
NEG_BIG = -1e30
LOG2E = math.log2(math.e)
HG_CHUNK = 128
HG_CHUNKS_PER_STEP = 4
PAGES_PER_STEP = 16
SAMPLE_ROWS_PER_STEP = 8
PAGES_PER_DOT = 4
VMEM_LIMIT = 48 * 1024 * 1024
SB_VMEM_LIMIT = 56 * 1024 * 1024

_NT = (((1,), (1,)), ((), ()))
_TN = (((0,), (0,)), ((), ()))


def _split_bf16(x):
    hi = x.astype(BF16)
    lo = (x - hi.astype(F32)).astype(BF16)
    return hi, lo


def _log_sigmoid_pair(t):
    sp = jnp.log1p(jnp.exp(-jnp.abs(t)))
    log_beta = jnp.minimum(t, 0.0) - sp
    return log_beta, log_beta - t


def _norm_proj_kernel(x_ref, g_ref, w_ref, *refs, outs, t_widths):
    wt_ref, out_refs = (refs[0], refs[1:]) if t_widths else (None, refs)
    x = x_ref[...]
    ms = jnp.mean(x * x, axis=-1, keepdims=True)
    xn = ((x * lax.rsqrt(ms + EPS)) * g_ref[...]).astype(BF16)
    cache = {}
    for o_ref, (lo, hi, scale) in zip(out_refs, outs):
        if (lo, hi) not in cache:
            cache[(lo, hi)] = jnp.dot(xn, w_ref[:, lo:hi], preferred_element_type=F32)
        h = cache[(lo, hi)]
        if scale != 1.0:
            h = h * scale
        o_ref[...] = h.astype(o_ref.dtype)
    off = 0
    for o_ref, width in zip(out_refs[len(outs):], t_widths):
        o_ref[0] = lax.dot_general(wt_ref[off:off + width, :], xn, _NT, preferred_element_type=F32)
        off += width


def _norm_proj(x3, gain, w_bf, outs, tm, wt_bf=None, t_widths=()):
    b, t, d = x3.shape
    m = b * t
    n = w_bf.shape[1]
    tm = min(tm, m)
    assert (t % tm == 0 or tm == m) and m % tm == 0
    nbt = max(t // tm, 1)
    assert (wt_bf is None) == (len(t_widths) == 0)
    t_args = [] if wt_bf is None else [wt_bf]
    kern = functools.partial(_norm_proj_kernel, outs=[(lo, hi, sc) for lo, hi, _, sc in outs], t_widths=tuple(t_widths))
    return pl.pallas_call(
        kern,
        grid=(m // tm,),
        in_specs=[pl.BlockSpec((tm, d), lambda i: (i, 0)),
                  pl.BlockSpec((1, d), lambda i: (0, 0)),
                  pl.BlockSpec((d, n), lambda i: (0, 0))]
                 + [pl.BlockSpec(a.shape, lambda i: (0, 0)) for a in t_args],
        out_specs=[pl.BlockSpec((tm, hi - lo), lambda i: (i, 0)) for lo, hi, _, _ in outs]
                  + [pl.BlockSpec((1, w, tm), lambda i: (i // nbt, 0, i % nbt)) for w in t_widths],
        out_shape=[jax.ShapeDtypeStruct((m, hi - lo), dt) for lo, hi, dt, _ in outs]
                  + [jax.ShapeDtypeStruct((b, w, t), F32) for w in t_widths],
        compiler_params=pltpu.CompilerParams(dimension_semantics=("arbitrary",), vmem_limit_bytes=VMEM_LIMIT),
    )(x3.reshape(m, d), gain.reshape(1, d), w_bf, *t_args)


def _suffix_matrix():
    jp = np.arange(LANES)[:, None]
    j = np.arange(LANES)[None, :]
    w = np.concatenate([(jp > j).astype(np.float32), np.ones((LANES, LANES), np.float32)], axis=1)
    return jnp.asarray(np.concatenate([w, w], axis=0), dtype=BF16)


def _sb_prompt_kernel(pt_ref, bias_ref, q_ref, kt_ref, vt_ref, w2_ref, *refs, npg, sample_steps):
    tq, tk = SB_TQ, SB_TK
    hp = pl.program_id(1)
    qi = pl.program_id(2)
    nkb = kt_ref.shape[2] // tk
    if npg:
        qs_ref, bias_s_ref, w2s_ref = refs[:3]
        page_refs = refs[3:3 + 2 * npg]
        o_ref, os_ref, kbd_ref, vbd_ref, carry_ref, acc_ref, lb_scr, x_scr, carry_s_ref, acc_s_ref = refs[3 + 2 * npg:]
        lin = (pl.program_id(0) * pl.num_programs(1) + hp) * pl.num_programs(2) + qi

        @pl.when(lin == 0)
        def _():
            carry_s_ref[...] = jnp.zeros_like(carry_s_ref)
            acc_s_ref[...] = jnp.zeros_like(acc_s_ref)

        sample = _sb_sample_parts(lin % sample_steps == 0, qs_ref, bias_s_ref, w2s_ref,
                                  page_refs[:npg], page_refs[npg:], os_ref, carry_s_ref, acc_s_ref)
    else:
        o_ref, kbd_ref, vbd_ref, carry_ref, acc_ref, lb_scr, x_scr = refs
        sample = None

    @pl.when(qi == 0)
    def _():
        row_v = lax.broadcasted_iota(jnp.int32, (LANES, tk), 0)
        for j in range(nkb):
            for src, dst in ((kt_ref, kbd_ref), (vt_ref, vbd_ref)):
                blk = src[0, :, j * tk:(j + 1) * tk].astype(BF16)
                zero = jnp.zeros_like(blk)
                dst[j, :, 0:tk] = jnp.where(row_v < SB_DH, blk, zero)
                dst[j, :, tk:2 * tk] = jnp.where(row_v >= SB_DH, blk, zero)

    q2 = q_ref[...]
    b2 = (bias_ref[2 * hp] * LOG2E, bias_ref[2 * hp + 1] * LOG2E)
    qpos = qi * tq + lax.broadcasted_iota(jnp.int32, (tq, LANES), 0)
    col = lax.broadcasted_iota(jnp.int32, (tq, LANES), 1)

    grp = tq // tk
    pend = 2

    def block_of(m, g):
        return (qi - m) * grp + (grp - 1 - g)

    def reset():
        carry_ref[...] = jnp.zeros_like(carry_ref)
        acc_ref[...] = jnp.zeros_like(acc_ref)

    def emit(tile):
        o_ref[pl.ds(pl.multiple_of(tile * tq, tq), tq), :] = acc_ref[...]

    def finish_scores(z, j, slot, g, masked):
        causal = (j * tk + col) < qpos if masked else None
        for half in range(2):
            rows = slice(half * tq, (half + 1) * tq)
            t = z[:, half * tk:(half + 1) * tk] + b2[half]
            sp = jnp.log(1.0 + jnp.exp2(-jnp.abs(t))) * LOG2E
            lb = jnp.minimum(t, 0.0) - sp
            l1m = lb - t
            if masked:
                l1m = jnp.where(causal, l1m, 0.0)
                lb = jnp.where(causal, lb, NEG_BIG)
            lb_scr[slot, g, rows, :] = lb
            x_scr[slot, g, rows, :] = l1m.astype(BF16)

    def finish_weights(r, j, slot, g):
        suffix = r[:, :LANES] + carry_ref[...]
        carry_ref[...] += r[:, LANES:]
        a = jnp.exp2(lb_scr[slot, g] + suffix).astype(BF16)
        a2 = jnp.concatenate([a[:tq], a[tq:]], axis=1)
        acc_ref[...] += lax.dot_general(a2, vbd_ref[j], _NT, preferred_element_type=F32)

    def stage(w_blocks, slot_w, m_s, slot_s, masked=False, between=None, rider=None):
        rs, zs = [], []
        for g in range(grp):
            if w_blocks is not None:
                rs.append(jnp.dot(x_scr[slot_w, g], w2_ref[...], preferred_element_type=F32))
            if m_s is not None:
                zs.append(jnp.dot(q2, kbd_ref[block_of(m_s, g)], preferred_element_type=F32))
        if rider is not None:
            n_pieces, r_scores, r_weights, r_begin, r_values, r_end = rider
            share = lambda g: range(g * n_pieces // grp, (g + 1) * n_pieces // grp)
            rz = []
        for g in range(grp):
            if w_blocks is not None:
                finish_weights(rs[g], w_blocks[g], slot_w, g)
            if rider is not None:
                for j in share(g):
                    rz += r_scores(j)
        if between is not None:
            between()
        if rider is not None:
            ra = r_weights(rz)
            racc = r_begin()
        for g in range(grp):
            if m_s is not None:
                finish_scores(zs[g], block_of(m_s, g), slot_s, g, masked)
            if rider is not None:
                for j in share(g):
                    racc = r_values(j, ra, racc)
        if rider is not None:
            r_end(racc)

    def blocks(m):
        return [block_of(m, g) for g in range(grp)]

    last_blocks = [grp - 1 - g for g in range(grp)]

    @pl.when(qi == 0)
    def _():
        reset()
        stage(None, None, 0, pend, masked=True, rider=sample)

    @pl.when(qi > 0)
    def _():
        def between():
            emit(qi - 1)
            reset()
        stage(last_blocks, pend, 0, 0, masked=True, between=between, rider=sample)

    def loop_body(p, c):
        stage(blocks(2 * p), 0, 2 * p + 1, 1)
        stage(blocks(2 * p + 1), 1, 2 * p + 2, 0)
        return c

    lax.fori_loop(0, (qi - 1) // 2, loop_body, 0)

    @pl.when(qi % 2 == 1)
    def _():
        stage(blocks(qi - 1), 0, qi, pend)

    @pl.when((qi % 2 == 0) & (qi > 0))
    def _():
        stage(blocks(qi - 2), 0, qi - 1, 1)
        stage(blocks(qi - 1), 1, qi, pend)

    @pl.when(qi == pl.num_programs(2) - 1)
    def _():
        stage(last_blocks, pend, None, None)
        emit(qi)


def _sb_prompt(q_bf, kt, vt, bias, sample=None):
    tq, tk = SB_TQ, SB_TK
    batch, _, seq = kt.shape
    assert seq % tq == 0 and tq % tk == 0
    nq = seq // tq
    npairs = SB_W // LANES
    const = lambda shape: pl.BlockSpec(shape, lambda b, h, i, pt: (0,) * len(shape))
    in_specs = [pl.BlockSpec(memory_space=pltpu.SMEM),
                pl.BlockSpec((tq, LANES), lambda b, h, i, pt: (b * nq + i, h)),
                pl.BlockSpec((1, LANES, seq), lambda b, h, i, pt: (b, h, 0)),
                pl.BlockSpec((1, LANES, seq), lambda b, h, i, pt: (b, h, 0)),
                const((LANES, 2 * LANES))]
    out_specs = [pl.BlockSpec((seq, LANES), lambda b, h, i, pt: (b, h))]
    out_shape = [jax.ShapeDtypeStruct((batch * seq, SB_W), F32)]
    scratch = [pltpu.VMEM((seq // tk, LANES, 2 * tk), BF16),
               pltpu.VMEM((seq // tk, LANES, 2 * tk), BF16),
               pltpu.VMEM((2 * tq, LANES), F32),
               pltpu.VMEM((tq, LANES), F32),
               pltpu.VMEM((3, tq // tk, 2 * tq, LANES), F32),
               pltpu.VMEM((3, tq // tk, 2 * tq, LANES), BF16)]
    args = [bias, q_bf, kt, vt, _suffix_matrix()[:LANES]]
    npg, sample_steps = 0, 1
    page_table = jnp.zeros((1,), jnp.int32)
    if sample is not None:
        qs_bf, bias_s, page_table, ckt, cvt = sample
        dec_b, n_pages = page_table.shape
        npg = PAGES_PER_STEP
        sample_steps = n_pages // npg
        page = ckt.shape[2]
        assert page == LANES and n_pages % npg == 0 and dec_b * sample_steps == batch * npairs * nq
        lin = lambda b, h, i: (b * npairs + h) * nq + i

        def page_spec(j):
            def index(b, h, i, pt):
                row, s = lin(b, h, i) // sample_steps, lin(b, h, i) % sample_steps
                return (pt[row * n_pages + (n_pages - 1 - (s * npg + j))], 0, 0)
            return pl.BlockSpec((1, SB_W, page), index)

        row_spec = pl.BlockSpec((1, 1, SB_W), lambda b, h, i, pt: (lin(b, h, i) // sample_steps, 0, 0))
        in_specs += [row_spec, const((SB_HEADS, LANES)), const((2 * LANES, 2 * LANES))]
        in_specs += [page_spec(j) for j in range(npg)] * 2
        out_specs.append(pl.BlockSpec((1, SB_W, 1), lambda b, h, i, pt: (lin(b, h, i) // sample_steps, 0, 0)))
        out_shape.append(jax.ShapeDtypeStruct((dec_b, SB_W, 1), F32))
        scratch += [pltpu.VMEM((SB_HEADS, LANES), F32), pltpu.VMEM((SB_W, LANES), F32)]
        args += [qs_bf.reshape(dec_b, 1, SB_W), jnp.broadcast_to(bias_s.astype(F32)[:, None], (SB_HEADS, LANES)),
                 _suffix_matrix()] + [ckt] * npg + [cvt] * npg
        page_table = page_table.reshape(-1)
    outs = pl.pallas_call(
        functools.partial(_sb_prompt_kernel, npg=npg, sample_steps=sample_steps),
        grid_spec=pltpu.PrefetchScalarGridSpec(num_scalar_prefetch=1, grid=(batch, npairs, nq), in_specs=in_specs,
                                               out_specs=out_specs, scratch_shapes=scratch),
        out_shape=out_shape,
        compiler_params=pltpu.CompilerParams(dimension_semantics=("arbitrary", "arbitrary", "arbitrary"),
                                             vmem_limit_bytes=SB_VMEM_LIMIT),
    )(page_table, *args)
    if sample is None:
        return outs[0], None
    return outs[0], outs[1].reshape(-1, SB_W)


def _hgrn_levels(c):
    return [c >> (i + 1) for i in range(int(math.log2(c)))]


def _hgrn_constants(c):
    t = np.arange(c)[:, None]
    u = np.arange(c)[None, :]
    rows = [(u <= t), (u > t)]
    masks = []
    for hs in _hgrn_levels(c):
        mid = (t // (2 * hs)) * (2 * hs) + hs - 1
        upper = (t % (2 * hs)) >= hs
        rows.append((upper & (u > mid) & (u <= t)) | ((~upper) & (u > t) & (u <= mid)))
        s = u
        masks.append(((t // (2 * hs)) == (s // (2 * hs))) & upper & ((s % (2 * hs)) < hs))
    masks.append(t == u)
    mall = np.concatenate(rows, axis=0).astype(np.float32)
    mall2 = np.concatenate([mall, mall], axis=1)
    return jnp.asarray(mall2, dtype=BF16), jnp.asarray(np.stack(masks).astype(np.float32))


def _forget_lower_bound(lbl_ref):
    l0 = lbl_ref[0:1]
    l1 = lbl_ref[1:2]
    m = jnp.maximum(l0, l1)
    e0 = jnp.exp(l0 - m)
    e1 = jnp.exp(l1 - m)
    return e0 / (e0 + e1)


def _hgrn_prompt_kernel(lbl_ref, hq_ref, hf_ref, hv_ref, mall_ref, bm_ref, o_ref, st_ref, s_scr, d_scr):
    c = HG_CHUNK
    ci = pl.program_id(1)
    levels = _hgrn_levels(c)
    n_sub = hq_ref.shape[1] // c

    @pl.when(ci == 0)
    def _():
        s_scr[...] = jnp.zeros_like(s_scr)

    lb = _forget_lower_bound(lbl_ref)
    lane = lax.broadcasted_iota(jnp.int32, (c, HG_W), 1)
    head_masks = [(lane >= h * HG_D) & (lane < (h + 1) * HG_D) for h in range(HG_HEADS)]
    r = lax.broadcasted_iota(jnp.int32, (HG_W, HG_W), 0)
    cc = lax.broadcasted_iota(jnp.int32, (HG_W, HG_W), 1)
    same_head = (r // HG_D) == (cc // HG_D)

    def by_head(x_bf):
        z = jnp.zeros_like(x_bf)
        return jnp.concatenate([jnp.where(m, x_bf, z) for m in head_masks], axis=0)

    def pair_scores(q_f32, k_f32, mask):
        p = lax.dot_general(q_f32.astype(BF16), by_head(k_f32.astype(BF16)), _NT, preferred_element_type=F32)
        return p * jnp.concatenate([mask] * HG_HEADS, axis=1)

    st = s_scr[...]
    for sub in range(n_sub):
        rows = slice(sub * c, (sub + 1) * c)
        d = d_scr.at[sub]
        q = hq_ref[0, rows, :]
        f = lb + (1.0 - lb) * jax.nn.sigmoid(hf_ref[0, rows, :])
        logf = jnp.log(f)
        k = 1.0 - f
        v_bf = hv_ref[0, rows, :].astype(BF16)
        hi, lo = _split_bf16(logf)
        d[...] = jnp.dot(mall_ref[...], jnp.concatenate([hi, lo], axis=0), preferred_element_type=F32)

        attn = pair_scores(q, k, bm_ref[len(levels)])
        for li in range(len(levels)):
            e = jnp.exp(d[(2 + li) * c:(3 + li) * c, :])
            attn = attn + pair_scores(q * e, k * e, bm_ref[li])
        o = jnp.dot(attn.astype(BF16), by_head(v_bf), preferred_element_type=F32)

        bcum = d[0:c, :]
        o = o + lax.dot_general((q * jnp.exp(bcum)).astype(BF16), st.astype(BF16), _NT,
                                preferred_element_type=F32)
        o_ref[0, rows, :] = o

        k_dec = (k * jnp.exp(d[c:2 * c, :])).astype(BF16)
        upd = lax.dot_general(v_bf, k_dec, _TN, preferred_element_type=F32)
        st = st * jnp.exp(bcum[c - 1:c, :]) + jnp.where(same_head, upd, 0.0)
    s_scr[...] = st

    @pl.when(ci == pl.num_programs(1) - 1)
    def _():
        st_ref[0] = st


def _hgrn_prompt(rest3, lb_logits):
    b, t, _ = rest3.shape
    c = HG_CHUNK
    mall2, bm = _hgrn_constants(c)
    nr = mall2.shape[0]
    n_sub = HG_CHUNKS_PER_STEP if t % (HG_CHUNKS_PER_STEP * c) == 0 else 1
    rows = n_sub * c
    col = lambda j: (lambda bi, ci: (bi, ci, j))
    return pl.pallas_call(
        _hgrn_prompt_kernel,
        grid=(b, t // rows),
        in_specs=[pl.BlockSpec((2, HG_W), lambda bi, ci: (0, 0)),
                  pl.BlockSpec((1, rows, HG_W), col(2)),
                  pl.BlockSpec((1, rows, HG_W), col(3)),
                  pl.BlockSpec((1, rows, HG_W), col(4)),
                  pl.BlockSpec((nr, 2 * c), lambda bi, ci: (0, 0)),
                  pl.BlockSpec(bm.shape, lambda bi, ci: (0, 0, 0))],
        out_specs=[pl.BlockSpec((1, rows, HG_W), lambda bi, ci: (bi, ci, 0)),
                   pl.BlockSpec((1, HG_W, HG_W), lambda bi, ci: (bi, 0, 0))],
        out_shape=[jax.ShapeDtypeStruct((b, t, HG_W), F32), jax.ShapeDtypeStruct((b, HG_W, HG_W), F32)],
        scratch_shapes=[pltpu.VMEM((HG_W, HG_W), F32), pltpu.VMEM((n_sub, nr, HG_W), F32)],
        compiler_params=pltpu.CompilerParams(dimension_semantics=("arbitrary", "arbitrary"),
                                             vmem_limit_bytes=VMEM_LIMIT),
    )(lb_logits, rest3, rest3, rest3, mall2, bm)


def _silu(g):
    return g * jax.nn.sigmoid(g)


def _mix_out_kernel(x_ref, sbo_ref, hgo_ref, sbg_ref, hgg_ref, xaq_ref, xag_ref, mkt_ref, mvt_ref,
                    wout_ref, hgn_ref, bd2_ref, fng_ref, y_ref):
    nb, tm = x_ref.shape[0], x_ref.shape[1]
    assert nb == 1 or tm == 1
    pad = nb == 1 and tm < 8

    def load(ref):
        if nb > 1:
            return ref[:, 0, :]
        a = ref[0]
        return jnp.broadcast_to(a, (8, a.shape[-1])) if pad else a

    x = load(x_ref)
    n_mem = mkt_ref.shape[2]
    row_m = lax.broadcasted_iota(jnp.int32, (XA_W, n_mem), 0)

    def cross_attend(xq_f32, mkt, mvt):
        r = xq_f32.shape[0]
        lane = lax.broadcasted_iota(jnp.int32, (r, XA_W), 1)
        xq = (xq_f32 * (1.0 / math.sqrt(XA_DH))).astype(BF16)
        zq = jnp.zeros_like(xq)
        q_st = jnp.concatenate([jnp.where((lane >= h * XA_DH) & (lane < (h + 1) * XA_DH), xq, zq)
                                for h in range(XA_HEADS)], axis=0)
        s = jnp.dot(q_st, mkt.astype(BF16), preferred_element_type=F32)
        p = jnp.exp(s - jnp.max(s, axis=-1, keepdims=True))
        p = (p / jnp.sum(p, axis=-1, keepdims=True)).astype(BF16)
        p2 = jnp.concatenate([p[h * r:(h + 1) * r] for h in range(XA_HEADS)], axis=1)
        mvt_bf = mvt.astype(BF16)
        zv = jnp.zeros_like(mvt_bf)
        mvt_st = jnp.concatenate([jnp.where((row_m >= h * XA_DH) & (row_m < (h + 1) * XA_DH), mvt_bf, zv)
                                  for h in range(XA_HEADS)], axis=1)
        return lax.dot_general(p2, mvt_st, _NT, preferred_element_type=F32)

    xaq = load(xaq_ref)
    if nb == 1:
        xa_o = cross_attend(xaq, mkt_ref[0], mvt_ref[0])
    else:
        xa_o = jnp.concatenate([cross_attend(jnp.broadcast_to(xaq[i:i + 1], (8, XA_W)), mkt_ref[i], mvt_ref[i])[0:1]
                                for i in range(nb)], axis=0)

    hg = load(hgo_ref)
    hi, lo = _split_bf16(hg * hg)
    ms = jnp.dot(jnp.concatenate([hi, lo], axis=1), bd2_ref[...], preferred_element_type=F32) * (1.0 / HG_D)
    hg_n = (hg * lax.rsqrt(ms + EPS)) * hgn_ref[...]

    sb = (load(sbo_ref) * _silu(load(sbg_ref))).astype(BF16)
    hgm = (hg_n * _silu(load(hgg_ref))).astype(BF16)
    xam = (xa_o * _silu(load(xag_ref))).astype(BF16)
    y = x + jnp.dot(sb, wout_ref[0:SB_W, :], preferred_element_type=F32)
    y = y + jnp.dot(hgm, wout_ref[SB_W:SB_W + HG_W, :], preferred_element_type=F32)
    y = y + jnp.dot(xam, wout_ref[SB_W + HG_W:, :], preferred_element_type=F32)
    msy = jnp.mean(y * y, axis=-1, keepdims=True)
    y = (y * lax.rsqrt(msy + EPS)) * fng_ref[...]
    if nb == 1:
        y_ref[0] = y[0:tm]
    else:
        y_ref[:, 0, :] = y


def _mix_out(x3, sbo3, hgo3, rest3, mkt3, mvt3, wout_bf, hg_norm_gain, final_gain, tm):
    b, t, d = x3.shape
    nm = mkt3.shape[2]
    nb = SAMPLE_ROWS_PER_STEP if (t == 1 and b % SAMPLE_ROWS_PER_STEP == 0) else 1
    assert t % tm == 0 and (nb == 1 or tm == 1)
    bd = (np.arange(HG_W)[:, None] // HG_D) == (np.arange(HG_W)[None, :] // HG_D)
    bd2 = jnp.asarray(np.concatenate([bd, bd], axis=0).astype(np.float32), dtype=BF16)
    blk = lambda w, j: pl.BlockSpec((nb, tm, w), lambda bi, ti: (bi, ti, j))
    const = lambda shape: pl.BlockSpec(shape, lambda bi, ti: (0,) * len(shape))
    return pl.pallas_call(
        _mix_out_kernel,
        grid=(b // nb, t // tm),
        in_specs=[blk(d, 0), blk(SB_W, 0), blk(HG_W, 0),
                  blk(SB_W, 0),
                  blk(HG_W, 5),
                  blk(XA_W, 6),
                  blk(XA_W, 7),
                  pl.BlockSpec((nb, XA_W, nm), lambda bi, ti: (bi, 0, 0)),
                  pl.BlockSpec((nb, XA_W, nm), lambda bi, ti: (bi, 0, 0)),
                  const(wout_bf.shape), const((1, HG_W)), const(bd2.shape), const((1, d))],
        out_specs=blk(d, 0),
        out_shape=jax.ShapeDtypeStruct((b, t, d), F32),
        compiler_params=pltpu.CompilerParams(dimension_semantics=("arbitrary", "arbitrary"),
                                             vmem_limit_bytes=VMEM_LIMIT),
    )(x3, sbo3, hgo3, rest3, rest3, rest3, rest3, mkt3, mvt3, wout_bf,
      hg_norm_gain.reshape(1, HG_W), bd2, final_gain.reshape(1, d))


def _sb_sample_parts(first, q_ref, bias_ref, w2_ref, kt_refs, vt_refs, o_ref, carry_ref, acc_ref):
    npg = len(kt_refs)
    row8 = lax.broadcasted_iota(jnp.int32, (SB_HEADS, SB_W), 0)
    lane8 = lax.broadcasted_iota(jnp.int32, (SB_HEADS, SB_W), 1)
    own = (lane8 // SB_DH) == row8

    grp = PAGES_PER_DOT
    assert npg % grp == 0

    def pages(refs, i0):
        return jnp.concatenate([refs[i][0] for i in range(i0, i0 + grp)], axis=1).astype(BF16)

    n_pieces = npg // grp

    def scores_piece(j):
        qb = jnp.broadcast_to(q_ref[0].astype(F32), (SB_HEADS, SB_W))
        q_bd = jnp.where(own, qb, 0.0).astype(BF16)
        zc = jnp.dot(q_bd, pages(kt_refs, j * grp), preferred_element_type=F32)
        return [zc[:, i * LANES:(i + 1) * LANES] for i in range(grp)]

    def weights(z_pages):
        z = jnp.concatenate(z_pages, axis=0)
        bias_t = jnp.concatenate([bias_ref[...]] * npg, axis=0)
        log_beta, log_1m = _log_sigmoid_pair(z + bias_t)
        hi, lo = _split_bf16(log_1m)
        r = jnp.dot(jnp.concatenate([hi, lo], axis=1), w2_ref[...], preferred_element_type=F32)
        run = jnp.where(first, 0.0, carry_ref[...])
        suffix = []
        for i in range(npg):
            rows = slice(i * SB_HEADS, (i + 1) * SB_HEADS)
            suffix.append(r[rows, :LANES] + run)
            run = run + r[rows, LANES:]
        carry_ref[...] = run
        return jnp.exp(log_beta + jnp.concatenate(suffix, axis=0))

    def values_begin():
        return jnp.where(first, 0.0, acc_ref[...])

    def values_piece(j, a, acc):
        for i in range(j * grp, (j + 1) * grp):
            a_rows = jnp.concatenate([jnp.broadcast_to(a[i * SB_HEADS + h:i * SB_HEADS + h + 1, :], (SB_DH, LANES))
                                      for h in range(SB_HEADS)], axis=0)
            acc = acc + vt_refs[i][0] * a_rows
        return acc

    def values_end(acc):
        acc_ref[...] = acc
        o_ref[0] = jnp.sum(acc, axis=1, keepdims=True)

    return n_pieces, scores_piece, weights, values_begin, values_piece, values_end


def _run_sample_parts(parts):
    n_pieces, scores_piece, weights, values_begin, values_piece, values_end = parts
    z = []
    for j in range(n_pieces):
        z += scores_piece(j)
    a = weights(z)
    acc = values_begin()
    for j in range(n_pieces):
        acc = values_piece(j, a, acc)
    values_end(acc)


def _sb_sample_kernel(pt_ref, q_ref, bias_ref, w2_ref, *refs):
    npg = PAGES_PER_STEP
    o_ref, carry_ref, acc_ref = refs[2 * npg:]

    @pl.when((pl.program_id(0) == 0) & (pl.program_id(1) == 0))
    def _():
        carry_ref[...] = jnp.zeros_like(carry_ref)
        acc_ref[...] = jnp.zeros_like(acc_ref)

    _run_sample_parts(_sb_sample_parts(pl.program_id(1) == 0, q_ref, bias_ref, w2_ref,
                                       refs[:npg], refs[npg:2 * npg], o_ref, carry_ref, acc_ref))


def _sb_sample(q_bf, bias, page_table, ckt, cvt):
    b = q_bf.shape[0]
    n_pages = page_table.shape[1]
    page = ckt.shape[2]
    assert page == LANES and n_pages % PAGES_PER_STEP == 0
    npg = PAGES_PER_STEP
    nsteps = n_pages // npg

    def page_spec(i):
        return pl.BlockSpec((1, SB_W, page),
                            lambda bi, si, pt: (pt[bi * n_pages + (n_pages - 1 - (si * npg + i))], 0, 0))

    grid_spec = pltpu.PrefetchScalarGridSpec(
        num_scalar_prefetch=1,
        grid=(b, nsteps),
        in_specs=[pl.BlockSpec((1, 1, SB_W), lambda bi, si, pt: (bi, 0, 0)),
                  pl.BlockSpec((SB_HEADS, LANES), lambda bi, si, pt: (0, 0)),
                  pl.BlockSpec((2 * LANES, 2 * LANES), lambda bi, si, pt: (0, 0))]
                 + [page_spec(i) for i in range(npg)] * 2,
        out_specs=pl.BlockSpec((1, SB_W, 1), lambda bi, si, pt: (bi, 0, 0)),
        scratch_shapes=[pltpu.VMEM((SB_HEADS, LANES), F32), pltpu.VMEM((SB_W, LANES), F32)])
    bias_rep = jnp.broadcast_to(bias.astype(F32)[:, None], (SB_HEADS, LANES))
    out = pl.pallas_call(
        _sb_sample_kernel,
        grid_spec=grid_spec,
        out_shape=jax.ShapeDtypeStruct((b, SB_W, 1), F32),
        compiler_params=pltpu.CompilerParams(dimension_semantics=("arbitrary", "arbitrary"),
                                             vmem_limit_bytes=VMEM_LIMIT),
    )(page_table.reshape(-1), q_bf.reshape(b, 1, SB_W), bias_rep, _suffix_matrix(),
      *([ckt] * npg), *([cvt] * npg))
    return out.reshape(b, SB_W)


def _hgrn_sample_kernel(lbl_ref, s_ref, q_ref, f_ref, v_ref, o_ref, sn_ref):
    l0 = lbl_ref[0]
    l1 = lbl_ref[1]
    m = jnp.maximum(l0, l1)
    e0 = jnp.exp(l0 - m)
    e1 = jnp.exp(l1 - m)
    lb = e0 / (e0 + e1)
    for r in range(s_ref.shape[0]):
        f = lb + (1.0 - lb) * jax.nn.sigmoid(f_ref[r])
        k = 1.0 - f
        v = v_ref[r]
        v_rows = jnp.concatenate([jnp.broadcast_to(v[:, h * HG_D:(h + 1) * HG_D], (HG_D, HG_D))
                                  for h in range(HG_HEADS)], axis=0)
        s_new = f * s_ref[r] + k * v_rows
        sn_ref[r] = s_new
        qs = q_ref[r] * s_new
        o_ref[r] = jnp.sum(qs.reshape(HG_HEADS, HG_D, HG_D), axis=1)


def _hgrn_sample(state, q_col, f_col, v_row, lb_logits):
    b = state.shape[0]
    rows = SAMPLE_ROWS_PER_STEP if b % SAMPLE_ROWS_PER_STEP == 0 else 1
    row = lambda shape: pl.BlockSpec((rows,) + shape, lambda bi: (bi, 0, 0))
    return pl.pallas_call(
        _hgrn_sample_kernel,
        grid=(b // rows,),
        in_specs=[pl.BlockSpec((2, HG_W, 1), lambda bi: (0, 0, 0)),
                  row((HG_W, HG_D)), row((HG_W, 1)), row((HG_W, 1)), row((1, HG_W))],
        out_specs=[row((HG_HEADS, HG_D)), row((HG_W, HG_D))],
        out_shape=[jax.ShapeDtypeStruct((b, HG_HEADS, HG_D), F32), jax.ShapeDtypeStruct((b, HG_W, HG_D), F32)],
        compiler_params=pltpu.CompilerParams(dimension_semantics=("arbitrary",)),
    )(lb_logits.reshape(2, HG_W, 1), state, q_col, f_col, v_row)


def _in_proj_outs(transposed_kv):
    qs = (LOG2E if transposed_kv else 1.0) / math.sqrt(SB_DH)
    q_out = (0, SB_W, BF16, qs)
    rest_out = (3 * SB_W, 3 * SB_W + D_REST, F32, 1.0)
    if transposed_kv:
        return [q_out, rest_out]
    return [q_out, (SB_W, 2 * SB_W, F32, 1.0), (2 * SB_W, 3 * SB_W, F32, 1.0), rest_out]


def kernel(x_prompt, x_sample, mem_prompt, cache_k, cache_v, page_table, state_hgrn, cache_mem_k, cache_mem_v,
           norm_gain, w_in, sb_bias, hg_lb_logits, hg_norm_gain, mem_norm_gain, w_mem_kv, w_out, final_norm_gain):
    batch, seq, d = x_prompt.shape
    dec_b = x_sample.shape[0]
    n_mem = mem_prompt.shape[1]
    depth = w_in.shape[0]
    assert depth == 1 and x_sample.shape[1] == 1
    l = 0
    lb_logits = hg_lb_logits.astype(F32)
    assert lb_logits.shape[0] == 2
    w_in_bf = w_in[l].astype(BF16)
    w_out_bf = w_out[l].astype(BF16)
    w_mem_bf = w_mem_kv[l].astype(BF16)
    bias = sb_bias[l].astype(F32)

    w_kv_t = w_in[l][:, SB_W:3 * SB_W].T.astype(BF16)
    q_bf, rest, kt_p, vt_p = _norm_proj(x_prompt, norm_gain[l], w_in_bf, _in_proj_outs(True), tm=256,
                                        wt_bf=w_kv_t, t_widths=(SB_W, SB_W))
    mkt, mvt = _norm_proj(mem_prompt, mem_norm_gain[l], w_mem_bf, [], tm=256,
                          wt_bf=w_mem_kv[l].T.astype(BF16), t_widths=(XA_W, XA_W))
    qs_bf, k_s, v_s, rest_s = _norm_proj(x_sample.reshape(1, dec_b, d), norm_gain[l], w_in_bf,
                                         _in_proj_outs(False), tm=dec_b)

    n_pool, page = cache_k.shape[1], cache_k.shape[2]
    ckt = jnp.transpose(cache_k[l], (0, 2, 3, 1)).reshape(n_pool, SB_W, page)
    cvt = jnp.transpose(cache_v[l], (0, 2, 3, 1)).reshape(n_pool, SB_W, page)
    n_pages = page_table.shape[1]
    fused = (n_pages % PAGES_PER_STEP == 0 and
             dec_b * (n_pages // PAGES_PER_STEP) == batch * (SB_W // LANES) * (seq // SB_TQ))
    if fused:
        sb_o, sb_o_s = _sb_prompt(q_bf, kt_p, vt_p, bias, sample=(qs_bf, bias, page_table, ckt, cvt))
    else:
        sb_o, _ = _sb_prompt(q_bf, kt_p, vt_p, bias)
        sb_o_s = _sb_sample(qs_bf, bias, page_table, ckt, cvt)

    rest3 = rest.reshape(batch, seq, D_REST)
    hg_o, st_t = _hgrn_prompt(rest3, lb_logits)
    y_prompt = _mix_out(x_prompt, sb_o.reshape(batch, seq, SB_W), hg_o, rest3, mkt, mvt,
                        w_out_bf, hg_norm_gain[l], final_norm_gain, tm=512)
    hgrn_prompt = jnp.stack([st_t[:, h * HG_D:(h + 1) * HG_D, h * HG_D:(h + 1) * HG_D]
                             for h in range(HG_HEADS)], axis=1).swapaxes(-1, -2)

    hq_s = rest_s[:, SB_W:SB_W + HG_W].reshape(dec_b, HG_W, 1)
    hf_s = rest_s[:, SB_W + HG_W:SB_W + 2 * HG_W].reshape(dec_b, HG_W, 1)
    hv_s = rest_s[:, SB_W + 2 * HG_W:SB_W + 3 * HG_W].reshape(dec_b, 1, HG_W)
    hg_o_s, st_s = _hgrn_sample(state_hgrn[l].astype(F32).reshape(dec_b, HG_W, HG_D), hq_s, hf_s, hv_s, lb_logits)
    y_sample = _mix_out(x_sample, sb_o_s.reshape(dec_b, 1, SB_W), hg_o_s.reshape(dec_b, 1, HG_W),
                        rest_s.reshape(dec_b, 1, D_REST),
                        jnp.transpose(cache_mem_k[l], (0, 2, 3, 1)).reshape(dec_b, XA_W, n_mem),
                        jnp.transpose(cache_mem_v[l], (0, 2, 3, 1)).reshape(dec_b, XA_W, n_mem),
                        w_out_bf, hg_norm_gain[l], final_norm_gain, tm=1)

    def token_major(a_t, heads, dh):
        b_, _, t_ = a_t.shape
        return jnp.transpose(a_t.reshape(b_, heads, dh, t_), (0, 3, 1, 2))[None]

    return (y_prompt, y_sample, token_major(kt_p, SB_HEADS, SB_DH), token_major(vt_p, SB_HEADS, SB_DH),
            hgrn_prompt.reshape(1, batch, HG_HEADS, HG_D, HG_D),
            token_major(mkt, XA_HEADS, XA_DH), token_major(mvt, XA_HEADS, XA_DH),
            k_s.reshape(1, dec_b, 1, SB_HEADS, SB_DH), v_s.reshape(1, dec_b, 1, SB_HEADS, SB_DH),
            st_s.reshape(1, dec_b, HG_HEADS, HG_D, HG_D).astype(state_hgrn.dtype))
```

```python
import functools
import math

import numpy as np
import jax
import jax.numpy as jnp
from jax import lax
from jax.experimental import pallas as pl
from jax.experimental.pallas import tpu as pltpu

F32 = jnp.float32
BF16 = jnp.bfloat16
EPS = 1e-6

SB_HEADS, SB_DH = 8, 64
HG_HEADS, HG_D = 4, 64
XA_HEADS, XA_DH = 4, 64
SB_W = SB_HEADS * SB_DH
HG_W = HG_HEADS * HG_D
XA_W = XA_HEADS * XA_DH
D_REST = SB_W + 3 * HG_W + HG_W + 2 * XA_W

LANES = 128
SB_TQ = 512
SB_TK = 128
NEG_BIG = -1e30
LOG2E = math.log2(math.e)
HG_CHUNK = 128
HG_CHUNKS_PER_STEP = 4
PAGES_PER_STEP = 16
SAMPLE_ROWS_PER_STEP = 8
PAGES_PER_DOT = 4
VMEM_LIMIT = 48 * 1024 * 1024
SB_VMEM_LIMIT = 56 * 1024 * 1024

_NT = (((1,), (1,)), ((), ()))
_TN = (((0,), (0,)), ((), ()))


def _split_bf16(x):
    hi = x.astype(BF16)
    lo = (x - hi.astype(F32)).astype(BF16)
    return hi, lo


def _log_sigmoid_pair(t):
    sp = jnp.log1p(jnp.exp(-jnp.abs(t)))
    log_beta = jnp.minimum(t, 0.0) - sp
    return log_beta, log_beta - t


def _norm_proj_kernel(x_ref, g_ref, w_ref, *refs, outs, t_widths):
    wt_ref, out_refs = (refs[0], refs[1:]) if t_widths else (None, refs)
    x = x_ref[...]
    ms = jnp.mean(x * x, axis=-1, keepdims=True)
    xn = ((x * lax.rsqrt(ms + EPS)) * g_ref[...]).astype(BF16)
    cache = {}
    for o_ref, (lo, hi, scale) in zip(out_refs, outs):
        if (lo, hi) not in cache:
            cache[(lo, hi)] = jnp.dot(xn, w_ref[:, lo:hi], preferred_element_type=F32)
        h = cache[(lo, hi)]
        if scale != 1.0:
            h = h * scale
        o_ref[...] = h.astype(o_ref.dtype)
    off = 0
    for o_ref, width in zip(out_refs[len(outs):], t_widths):
        o_ref[0] = lax.dot_general(wt_ref[off:off + width, :], xn, _NT, preferred_element_type=F32)
        off += width


def _norm_proj(x3, gain, w_bf, outs, tm, wt_bf=None, t_widths=()):
    b, t, d = x3.shape
    m = b * t
    n = w_bf.shape[1]
    tm = min(tm, m)
    assert (t % tm == 0 or tm == m) and m % tm == 0
    nbt = max(t // tm, 1)
    assert (wt_bf is None) == (len(t_widths) == 0)
    t_args = [] if wt_bf is None else [wt_bf]
    kern = functools.partial(_norm_proj_kernel, outs=[(lo, hi, sc) for lo, hi, _, sc in outs], t_widths=tuple(t_widths))
    return pl.pallas_call(
        kern,
        grid=(m // tm,),
        in_specs=[pl.BlockSpec((tm, d), lambda i: (i, 0)),
                  pl.BlockSpec((1, d), lambda i: (0, 0)),
                  pl.BlockSpec((d, n), lambda i: (0, 0))]
                 + [pl.BlockSpec(a.shape, lambda i: (0, 0)) for a in t_args],
        out_specs=[pl.BlockSpec((tm, hi - lo), lambda i: (i, 0)) for lo, hi, _, _ in outs]
                  + [pl.BlockSpec((1, w, tm), lambda i: (i // nbt, 0, i % nbt)) for w in t_widths],
        out_shape=[jax.ShapeDtypeStruct((m, hi - lo), dt) for lo, hi, dt, _ in outs]
                  + [jax.ShapeDtypeStruct((b, w, t), F32) for w in t_widths],
        compiler_params=pltpu.CompilerParams(dimension_semantics=("arbitrary",), vmem_limit_bytes=VMEM_LIMIT),
    )(x3.reshape(m, d), gain.reshape(1, d), w_bf, *t_args)


def _suffix_matrix():
    jp = np.arange(LANES)[:, None]
    j = np.arange(LANES)[None, :]
    w = np.concatenate([(jp > j).astype(np.float32), np.ones((LANES, LANES), np.float32)], axis=1)
    return jnp.asarray(np.concatenate([w, w], axis=0), dtype=BF16)


def _sb_prompt_kernel(pt_ref, bias_ref, q_ref, kt_ref, vt_ref, w2_ref, *refs, npg, sample_steps):
    tq, tk = SB_TQ, SB_TK
    hp = pl.program_id(1)
    qi = pl.program_id(2)
    nkb = kt_ref.shape[2] // tk
    if npg:
        qs_ref, bias_s_ref, w2s_ref = refs[:3]
        page_refs = refs[3:3 + 2 * npg]
        o_ref, os_ref, kbd_ref, vbd_ref, carry_ref, acc_ref, lb_scr, x_scr, carry_s_ref, acc_s_ref = refs[3 + 2 * npg:]
        lin = (pl.program_id(0) * pl.num_programs(1) + hp) * pl.num_programs(2) + qi

        @pl.when(lin == 0)
        def _():
            carry_s_ref[...] = jnp.zeros_like(carry_s_ref)
            acc_s_ref[...] = jnp.zeros_like(acc_s_ref)

        sample = _sb_sample_parts(lin % sample_steps == 0, qs_ref, bias_s_ref, w2s_ref,
                                  page_refs[:npg], page_refs[npg:], os_ref, carry_s_ref, acc_s_ref)
    else:
        o_ref, kbd_ref, vbd_ref, carry_ref, acc_ref, lb_scr, x_scr = refs
        sample = None

    @pl.when(qi == 0)
    def _():
        row_v = lax.broadcasted_iota(jnp.int32, (LANES, tk), 0)
        for j in range(nkb):
            for src, dst in ((kt_ref, kbd_ref), (vt_ref, vbd_ref)):
                blk = src[0, :, j * tk:(j + 1) * tk].astype(BF16)
                zero = jnp.zeros_like(blk)
                dst[j, :, 0:tk] = jnp.where(row_v < SB_DH, blk, zero)
                dst[j, :, tk:2 * tk] = jnp.where(row_v >= SB_DH, blk, zero)

    q2 = q_ref[...]
    b2 = (bias_ref[2 * hp] * LOG2E, bias_ref[2 * hp + 1] * LOG2E)

    grp = tq // tk
    pend = 2

    def block_of(m, g):
        return (qi - m) * grp + (grp - 1 - g)

    def reset():
        carry_ref[...] = jnp.zeros_like(carry_ref)
        acc_ref[...] = jnp.zeros_like(acc_ref)

    def emit(tile):
        o_ref[pl.ds(pl.multiple_of(tile * tq, tq), tq), :] = acc_ref[...]

    def finish_scores(z, j, slot, g, masked):
        r0 = (grp - 1 - g) * tk if masked else 0
        if masked:
            q_pos = qi * tq + r0 + lax.broadcasted_iota(jnp.int32, (tq - r0, LANES), 0)
            k_pos = j * tk + lax.broadcasted_iota(jnp.int32, (tq - r0, LANES), 1)
            causal = k_pos < q_pos
        for half in range(2):
            t = z[r0:, half * tk:(half + 1) * tk] + b2[half]
            sp = jnp.log(1.0 + jnp.exp2(-jnp.abs(t))) * LOG2E
            lb = jnp.minimum(t, 0.0) - sp
            l1m = lb - t
            if masked:
                l1m = jnp.where(causal, l1m, 0.0)
                lb = jnp.where(causal, lb, NEG_BIG)
            lb_scr[slot, g, half * tq + r0:(half + 1) * tq, :] = lb
            x_scr[slot, g, half * tq + r0:(half + 1) * tq, :] = l1m.astype(BF16)
            if r0:
                lb_scr[slot, g, half * tq:half * tq + r0, :] = jnp.full((r0, LANES), NEG_BIG, F32)
                x_scr[slot, g, half * tq:half * tq + r0, :] = jnp.zeros((r0, LANES), BF16)

    def finish_weights(r, j, slot, g):
        suffix = r[:, :LANES] + carry_ref[...]
        carry_ref[...] += r[:, LANES:]
        a = jnp.exp2(lb_scr[slot, g] + suffix).astype(BF16)
        a2 = jnp.concatenate([a[:tq], a[tq:]], axis=1)
        acc_ref[...] += lax.dot_general(a2, vbd_ref[j], _NT, preferred_element_type=F32)

    def stage(w_blocks, slot_w, m_s, slot_s, masked=False, between=None, rider=None):
        rs, zs = [], []
        for g in range(grp):
            if w_blocks is not None:
                rs.append(jnp.dot(x_scr[slot_w, g], w2_ref[...], preferred_element_type=F32))
            if m_s is not None:
                zs.append(jnp.dot(q2, kbd_ref[block_of(m_s, g)], preferred_element_type=F32))
        if rider is not None:
            n_pieces, r_scores, r_weights, r_begin, r_values, r_end = rider
            share = lambda g: range(g * n_pieces // grp, (g + 1) * n_pieces // grp)
            rz = []
        for g in range(grp):
            if w_blocks is not None:
                finish_weights(rs[g], w_blocks[g], slot_w, g)
            if rider is not None:
                for j in share(g):
                    rz += r_scores(j)
        if between is not None:
            between()
        if rider is not None:
            ra = r_weights(rz)
            racc = r_begin()
        for g in range(grp):
            if m_s is not None:
                finish_scores(zs[g], block_of(m_s, g), slot_s, g, masked)
            if rider is not None:
                for j in share(g):
                    racc = r_values(j, ra, racc)
        if rider is not None:
            r_end(racc)

    def blocks(m):
        return [block_of(m, g) for g in range(grp)]

    last_blocks = [grp - 1 - g for g in range(grp)]

    @pl.when(qi == 0)
    def _():
        reset()
        stage(None, None, 0, pend, masked=True, rider=sample)

    @pl.when(qi > 0)
    def _():
        def between():
            emit(qi - 1)
            reset()
        stage(last_blocks, pend, 0, 0, masked=True, between=between, rider=sample)

    def loop_body(p, c):
        stage(blocks(2 * p), 0, 2 * p + 1, 1)
        stage(blocks(2 * p + 1), 1, 2 * p + 2, 0)
        return c

    lax.fori_loop(0, (qi - 1) // 2, loop_body, 0)

    @pl.when(qi % 2 == 1)
    def _():
        stage(blocks(qi - 1), 0, qi, pend)

    @pl.when((qi % 2 == 0) & (qi > 0))
    def _():
        stage(blocks(qi - 2), 0, qi - 1, 1)
        stage(blocks(qi - 1), 1, qi, pend)

    @pl.when(qi == pl.num_programs(2) - 1)
    def _():
        stage(last_blocks, pend, None, None)
        emit(qi)


def _sb_prompt(q_bf, kt, vt, bias, sample=None):
    tq, tk = SB_TQ, SB_TK
    batch, _, seq = kt.shape
    assert seq % tq == 0 and tq % tk == 0
    nq = seq // tq
    npairs = SB_W // LANES
    const = lambda shape: pl.BlockSpec(shape, lambda b, h, i, pt: (0,) * len(shape))
    in_specs = [pl.BlockSpec(memory_space=pltpu.SMEM),
                pl.BlockSpec((tq, LANES), lambda b, h, i, pt: (b * nq + i, h)),
                pl.BlockSpec((1, LANES, seq), lambda b, h, i, pt: (b, h, 0)),
                pl.BlockSpec((1, LANES, seq), lambda b, h, i, pt: (b, h, 0)),
                const((LANES, 2 * LANES))]
    out_specs = [pl.BlockSpec((seq, LANES), lambda b, h, i, pt: (b, h))]
    out_shape = [jax.ShapeDtypeStruct((batch * seq, SB_W), F32)]
    scratch = [pltpu.VMEM((seq // tk, LANES, 2 * tk), BF16),
               pltpu.VMEM((seq // tk, LANES, 2 * tk), BF16),
               pltpu.VMEM((2 * tq, LANES), F32),
               pltpu.VMEM((tq, LANES), F32),
               pltpu.VMEM((3, tq // tk, 2 * tq, LANES), F32),
               pltpu.VMEM((3, tq // tk, 2 * tq, LANES), BF16)]
    args = [bias, q_bf, kt, vt, _suffix_matrix()[:LANES]]
    npg, sample_steps = 0, 1
    page_table = jnp.zeros((1,), jnp.int32)
    if sample is not None:
        qs_bf, bias_s, page_table, ckt, cvt = sample
        dec_b, n_pages = page_table.shape
        npg = PAGES_PER_STEP
        sample_steps = n_pages // npg
        page = ckt.shape[2]
        assert page == LANES and n_pages % npg == 0 and dec_b * sample_steps == batch * npairs * nq
        lin = lambda b, h, i: (b * npairs + h) * nq + i

        def page_spec(j):
            def index(b, h, i, pt):
                row, s = lin(b, h, i) // sample_steps, lin(b, h, i) % sample_steps
                return (pt[row * n_pages + (n_pages - 1 - (s * npg + j))], 0, 0)
            return pl.BlockSpec((1, SB_W, page), index)

        row_spec = pl.BlockSpec((1, 1, SB_W), lambda b, h, i, pt: (lin(b, h, i) // sample_steps, 0, 0))
        in_specs += [row_spec, const((SB_HEADS, LANES)), const((2 * LANES, 2 * LANES))]
        in_specs += [page_spec(j) for j in range(npg)] * 2
        out_specs.append(pl.BlockSpec((1, SB_W, 1), lambda b, h, i, pt: (lin(b, h, i) // sample_steps, 0, 0)))
        out_shape.append(jax.ShapeDtypeStruct((dec_b, SB_W, 1), F32))
        scratch += [pltpu.VMEM((SB_HEADS, LANES), F32), pltpu.VMEM((SB_W, LANES), F32)]
        args += [qs_bf.reshape(dec_b, 1, SB_W), jnp.broadcast_to(bias_s.astype(F32)[:, None], (SB_HEADS, LANES)),
                 _suffix_matrix()] + [ckt] * npg + [cvt] * npg
        page_table = page_table.reshape(-1)
    outs = pl.pallas_call(
        functools.partial(_sb_prompt_kernel, npg=npg, sample_steps=sample_steps),
        grid_spec=pltpu.PrefetchScalarGridSpec(num_scalar_prefetch=1, grid=(batch, npairs, nq), in_specs=in_specs,
                                               out_specs=out_specs, scratch_shapes=scratch),
        out_shape=out_shape,
        compiler_params=pltpu.CompilerParams(dimension_semantics=("arbitrary", "arbitrary", "arbitrary"),
                                             vmem_limit_bytes=SB_VMEM_LIMIT),
    )(page_table, *args)
    if sample is None:
        return outs[0], None
    return outs[0], outs[1].reshape(-1, SB_W)


def _hgrn_levels(c):
    return [c >> (i + 1) for i in range(int(math.log2(c)))]


def _hgrn_constants(c):
    t = np.arange(c)[:, None]
    u = np.arange(c)[None, :]
    rows = [(u <= t), (u > t)]
    masks = []
    for hs in _hgrn_levels(c):
        mid = (t // (2 * hs)) * (2 * hs) + hs - 1
        upper = (t % (2 * hs)) >= hs
        rows.append((upper & (u > mid) & (u <= t)) | ((~upper) & (u > t) & (u <= mid)))
        s = u
        masks.append(((t // (2 * hs)) == (s // (2 * hs))) & upper & ((s % (2 * hs)) < hs))
    masks.append(t == u)
    mall = np.concatenate(rows, axis=0).astype(np.float32)
    mall2 = np.concatenate([mall, mall], axis=1)
    return jnp.asarray(mall2, dtype=BF16), jnp.asarray(np.stack(masks).astype(np.float32))


def _forget_lower_bound(lbl_ref):
    l0 = lbl_ref[0:1]
    l1 = lbl_ref[1:2]
    m = jnp.maximum(l0, l1)
    e0 = jnp.exp(l0 - m)
    e1 = jnp.exp(l1 - m)
    return e0 / (e0 + e1)


def _hgrn_prompt_kernel(lbl_ref, hq_ref, hf_ref, hv_ref, mall_ref, bm_ref, o_ref, st_ref, s_scr, d_scr):
    c = HG_CHUNK
    ci = pl.program_id(1)
    levels = _hgrn_levels(c)
    n_sub = hq_ref.shape[1] // c

    @pl.when(ci == 0)
    def _():
        s_scr[...] = jnp.zeros_like(s_scr)

    lb = _forget_lower_bound(lbl_ref)
    lane = lax.broadcasted_iota(jnp.int32, (c, HG_W), 1)
    head_masks = [(lane >= h * HG_D) & (lane < (h + 1) * HG_D) for h in range(HG_HEADS)]
    r = lax.broadcasted_iota(jnp.int32, (HG_W, HG_W), 0)
    cc = lax.broadcasted_iota(jnp.int32, (HG_W, HG_W), 1)
    same_head = (r // HG_D) == (cc // HG_D)

    def by_head(x_bf):
        z = jnp.zeros_like(x_bf)
        return jnp.concatenate([jnp.where(m, x_bf, z) for m in head_masks], axis=0)

    def pair_scores(q_f32, k_f32, mask):
        p = lax.dot_general(q_f32.astype(BF16), by_head(k_f32.astype(BF16)), _NT, preferred_element_type=F32)
        return p * jnp.concatenate([mask] * HG_HEADS, axis=1)

    st = s_scr[...]
    for sub in range(n_sub):
        rows = slice(sub * c, (sub + 1) * c)
        d = d_scr.at[sub]
        q = hq_ref[0, rows, :]
        f = lb + (1.0 - lb) * jax.nn.sigmoid(hf_ref[0, rows, :])
        logf = jnp.log(f)
        k = 1.0 - f
        v_bf = hv_ref[0, rows, :].astype(BF16)
        hi, lo = _split_bf16(logf)
        d[...] = jnp.dot(mall_ref[...], jnp.concatenate([hi, lo], axis=0), preferred_element_type=F32)

        attn = pair_scores(q, k, bm_ref[len(levels)])
        for li in range(len(levels)):
            e = jnp.exp(d[(2 + li) * c:(3 + li) * c, :])
            attn = attn + pair_scores(q * e, k * e, bm_ref[li])
        o = jnp.dot(attn.astype(BF16), by_head(v_bf), preferred_element_type=F32)

        bcum = d[0:c, :]
        o = o + lax.dot_general((q * jnp.exp(bcum)).astype(BF16), st.astype(BF16), _NT,
                                preferred_element_type=F32)
        o_ref[0, rows, :] = o

        k_dec = (k * jnp.exp(d[c:2 * c, :])).astype(BF16)
        upd = lax.dot_general(v_bf, k_dec, _TN, preferred_element_type=F32)
        st = st * jnp.exp(bcum[c - 1:c, :]) + jnp.where(same_head, upd, 0.0)
    s_scr[...] = st

    @pl.when(ci == pl.num_programs(1) - 1)
    def _():
        st_ref[0] = st


def _hgrn_prompt(rest3, lb_logits):
    b, t, _ = rest3.shape
    c = HG_CHUNK
    mall2, bm = _hgrn_constants(c)
    nr = mall2.shape[0]
    n_sub = HG_CHUNKS_PER_STEP if t % (HG_CHUNKS_PER_STEP * c) == 0 else 1
    rows = n_sub * c
    col = lambda j: (lambda bi, ci: (bi, ci, j))
    return pl.pallas_call(
        _hgrn_prompt_kernel,
        grid=(b, t // rows),
        in_specs=[pl.BlockSpec((2, HG_W), lambda bi, ci: (0, 0)),
                  pl.BlockSpec((1, rows, HG_W), col(2)),
                  pl.BlockSpec((1, rows, HG_W), col(3)),
                  pl.BlockSpec((1, rows, HG_W), col(4)),
                  pl.BlockSpec((nr, 2 * c), lambda bi, ci: (0, 0)),
                  pl.BlockSpec(bm.shape, lambda bi, ci: (0, 0, 0))],
        out_specs=[pl.BlockSpec((1, rows, HG_W), lambda bi, ci: (bi, ci, 0)),
                   pl.BlockSpec((1, HG_W, HG_W), lambda bi, ci: (bi, 0, 0))],
        out_shape=[jax.ShapeDtypeStruct((b, t, HG_W), F32), jax.ShapeDtypeStruct((b, HG_W, HG_W), F32)],
        scratch_shapes=[pltpu.VMEM((HG_W, HG_W), F32), pltpu.VMEM((n_sub, nr, HG_W), F32)],
        compiler_params=pltpu.CompilerParams(dimension_semantics=("arbitrary", "arbitrary"),
                                             vmem_limit_bytes=VMEM_LIMIT),
    )(lb_logits, rest3, rest3, rest3, mall2, bm)


def _silu(g):
    return g * jax.nn.sigmoid(g)


def _mix_out_kernel(x_ref, sbo_ref, hgo_ref, sbg_ref, hgg_ref, xaq_ref, xag_ref, mkt_ref, mvt_ref,
                    wout_ref, hgn_ref, bd2_ref, fng_ref, y_ref):
    nb, tm = x_ref.shape[0], x_ref.shape[1]
    assert nb == 1 or tm == 1
    pad = nb == 1 and tm < 8

    def load(ref):
        if nb > 1:
            return ref[:, 0, :]
        a = ref[0]
        return jnp.broadcast_to(a, (8, a.shape[-1])) if pad else a

    x = load(x_ref)
    n_mem = mkt_ref.shape[2]
    row_m = lax.broadcasted_iota(jnp.int32, (XA_W, n_mem), 0)

    def cross_attend(xq_f32, mkt, mvt):
        r = xq_f32.shape[0]
        lane = lax.broadcasted_iota(jnp.int32, (r, XA_W), 1)
        xq = (xq_f32 * (1.0 / math.sqrt(XA_DH))).astype(BF16)
        zq = jnp.zeros_like(xq)
        q_st = jnp.concatenate([jnp.where((lane >= h * XA_DH) & (lane < (h + 1) * XA_DH), xq, zq)
                                for h in range(XA_HEADS)], axis=0)
        s = jnp.dot(q_st, mkt.astype(BF16), preferred_element_type=F32)
        p = jnp.exp(s - jnp.max(s, axis=-1, keepdims=True))
        p = (p / jnp.sum(p, axis=-1, keepdims=True)).astype(BF16)
        p2 = jnp.concatenate([p[h * r:(h + 1) * r] for h in range(XA_HEADS)], axis=1)
        mvt_bf = mvt.astype(BF16)
        zv = jnp.zeros_like(mvt_bf)
        mvt_st = jnp.concatenate([jnp.where((row_m >= h * XA_DH) & (row_m < (h + 1) * XA_DH), mvt_bf, zv)
                                  for h in range(XA_HEADS)], axis=1)
        return lax.dot_general(p2, mvt_st, _NT, preferred_element_type=F32)

    xaq = load(xaq_ref)
    if nb == 1:
        xa_o = cross_attend(xaq, mkt_ref[0], mvt_ref[0])
    else:
        xa_o = jnp.concatenate([cross_attend(jnp.broadcast_to(xaq[i:i + 1], (8, XA_W)), mkt_ref[i], mvt_ref[i])[0:1]
                                for i in range(nb)], axis=0)

    hg = load(hgo_ref)
    hi, lo = _split_bf16(hg * hg)
    ms = jnp.dot(jnp.concatenate([hi, lo], axis=1), bd2_ref[...], preferred_element_type=F32) * (1.0 / HG_D)
    hg_n = (hg * lax.rsqrt(ms + EPS)) * hgn_ref[...]

    sb = (load(sbo_ref) * _silu(load(sbg_ref))).astype(BF16)
    hgm = (hg_n * _silu(load(hgg_ref))).astype(BF16)
    xam = (xa_o * _silu(load(xag_ref))).astype(BF16)
    y = x + jnp.dot(sb, wout_ref[0:SB_W, :], preferred_element_type=F32)
    y = y + jnp.dot(hgm, wout_ref[SB_W:SB_W + HG_W, :], preferred_element_type=F32)
    y = y + jnp.dot(xam, wout_ref[SB_W + HG_W:, :], preferred_element_type=F32)
    msy = jnp.mean(y * y, axis=-1, keepdims=True)
    y = (y * lax.rsqrt(msy + EPS)) * fng_ref[...]
    if nb == 1:
        y_ref[0] = y[0:tm]
    else:
        y_ref[:, 0, :] = y


def _mix_out(x3, sbo3, hgo3, rest3, mkt3, mvt3, wout_bf, hg_norm_gain, final_gain, tm):
    b, t, d = x3.shape
    nm = mkt3.shape[2]
    nb = SAMPLE_ROWS_PER_STEP if (t == 1 and b % SAMPLE_ROWS_PER_STEP == 0) else 1
    assert t % tm == 0 and (nb == 1 or tm == 1)
    bd = (np.arange(HG_W)[:, None] // HG_D) == (np.arange(HG_W)[None, :] // HG_D)
    bd2 = jnp.asarray(np.concatenate([bd, bd], axis=0).astype(np.float32), dtype=BF16)
    blk = lambda w, j: pl.BlockSpec((nb, tm, w), lambda bi, ti: (bi, ti, j))
    const = lambda shape: pl.BlockSpec(shape, lambda bi, ti: (0,) * len(shape))
    return pl.pallas_call(
        _mix_out_kernel,
        grid=(b // nb, t // tm),
        in_specs=[blk(d, 0), blk(SB_W, 0), blk(HG_W, 0),
                  blk(SB_W, 0),
                  blk(HG_W, 5),
                  blk(XA_W, 6),
                  blk(XA_W, 7),
                  pl.BlockSpec((nb, XA_W, nm), lambda bi, ti: (bi, 0, 0)),
                  pl.BlockSpec((nb, XA_W, nm), lambda bi, ti: (bi, 0, 0)),
                  const(wout_bf.shape), const((1, HG_W)), const(bd2.shape), const((1, d))],
        out_specs=blk(d, 0),
        out_shape=jax.ShapeDtypeStruct((b, t, d), F32),
        compiler_params=pltpu.CompilerParams(dimension_semantics=("arbitrary", "arbitrary"),
                                             vmem_limit_bytes=VMEM_LIMIT),
    )(x3, sbo3, hgo3, rest3, rest3, rest3, rest3, mkt3, mvt3, wout_bf,
      hg_norm_gain.reshape(1, HG_W), bd2, final_gain.reshape(1, d))


def _sb_sample_parts(first, q_ref, bias_ref, w2_ref, kt_refs, vt_refs, o_ref, carry_ref, acc_ref):
    npg = len(kt_refs)
    row8 = lax.broadcasted_iota(jnp.int32, (SB_HEADS, SB_W), 0)
    lane8 = lax.broadcasted_iota(jnp.int32, (SB_HEADS, SB_W), 1)
    own = (lane8 // SB_DH) == row8

    grp = PAGES_PER_DOT
    assert npg % grp == 0

    def pages(refs, i0):
        return jnp.concatenate([refs[i][0] for i in range(i0, i0 + grp)], axis=1).astype(BF16)

    n_pieces = npg // grp

    def scores_piece(j):
        qb = jnp.broadcast_to(q_ref[0].astype(F32), (SB_HEADS, SB_W))
        q_bd = jnp.where(own, qb, 0.0).astype(BF16)
        zc = jnp.dot(q_bd, pages(kt_refs, j * grp), preferred_element_type=F32)
        return [zc[:, i * LANES:(i + 1) * LANES] for i in range(grp)]

    def weights(z_pages):
        z = jnp.concatenate(z_pages, axis=0)
        bias_t = jnp.concatenate([bias_ref[...]] * npg, axis=0)
        log_beta, log_1m = _log_sigmoid_pair(z + bias_t)
        hi, lo = _split_bf16(log_1m)
        r = jnp.dot(jnp.concatenate([hi, lo], axis=1), w2_ref[...], preferred_element_type=F32)
        run = jnp.where(first, 0.0, carry_ref[...])
        suffix = []
        for i in range(npg):
            rows = slice(i * SB_HEADS, (i + 1) * SB_HEADS)
            suffix.append(r[rows, :LANES] + run)
            run = run + r[rows, LANES:]
        carry_ref[...] = run
        return jnp.exp(log_beta + jnp.concatenate(suffix, axis=0))

    def values_begin():
        return jnp.where(first, 0.0, acc_ref[...])

    def values_piece(j, a, acc):
        for i in range(j * grp, (j + 1) * grp):
            a_rows = jnp.concatenate([jnp.broadcast_to(a[i * SB_HEADS + h:i * SB_HEADS + h + 1, :], (SB_DH, LANES))
                                      for h in range(SB_HEADS)], axis=0)
            acc = acc + vt_refs[i][0] * a_rows
        return acc

    def values_end(acc):
        acc_ref[...] = acc
        o_ref[0] = jnp.sum(acc, axis=1, keepdims=True)

    return n_pieces, scores_piece, weights, values_begin, values_piece, values_end


def _run_sample_parts(parts):
    n_pieces, scores_piece, weights, values_begin, values_piece, values_end = parts
    z = []
    for j in range(n_pieces):
        z += scores_piece(j)
    a = weights(z)
    acc = values_begin()
    for j in range(n_pieces):
        acc = values_piece(j, a, acc)
    values_end(acc)


def _sb_sample_kernel(pt_ref, q_ref, bias_ref, w2_ref, *refs):
    npg = PAGES_PER_STEP
    o_ref, carry_ref, acc_ref = refs[2 * npg:]

    @pl.when((pl.program_id(0) == 0) & (pl.program_id(1) == 0))
    def _():
        carry_ref[...] = jnp.zeros_like(carry_ref)
        acc_ref[...] = jnp.zeros_like(acc_ref)

    _run_sample_parts(_sb_sample_parts(pl.program_id(1) == 0, q_ref, bias_ref, w2_ref,
                                       refs[:npg], refs[npg:2 * npg], o_ref, carry_ref, acc_ref))


def _sb_sample(q_bf, bias, page_table, ckt, cvt):
    b = q_bf.shape[0]
    n_pages = page_table.shape[1]
    page = ckt.shape[2]
    assert page == LANES and n_pages % PAGES_PER_STEP == 0
    npg = PAGES_PER_STEP
    nsteps = n_pages // npg

    def page_spec(i):
        return pl.BlockSpec((1, SB_W, page),
                            lambda bi, si, pt: (pt[bi * n_pages + (n_pages - 1 - (si * npg + i))], 0, 0))

    grid_spec = pltpu.PrefetchScalarGridSpec(
        num_scalar_prefetch=1,
        grid=(b, nsteps),
        in_specs=[pl.BlockSpec((1, 1, SB_W), lambda bi, si, pt: (bi, 0, 0)),
                  pl.BlockSpec((SB_HEADS, LANES), lambda bi, si, pt: (0, 0)),
                  pl.BlockSpec((2 * LANES, 2 * LANES), lambda bi, si, pt: (0, 0))]
                 + [page_spec(i) for i in range(npg)] * 2,
        out_specs=pl.BlockSpec((1, SB_W, 1), lambda bi, si, pt: (bi, 0, 0)),
        scratch_shapes=[pltpu.VMEM((SB_HEADS, LANES), F32), pltpu.VMEM((SB_W, LANES), F32)])
    bias_rep = jnp.broadcast_to(bias.astype(F32)[:, None], (SB_HEADS, LANES))
    out = pl.pallas_call(
        _sb_sample_kernel,
        grid_spec=grid_spec,
        out_shape=jax.ShapeDtypeStruct((b, SB_W, 1), F32),
        compiler_params=pltpu.CompilerParams(dimension_semantics=("arbitrary", "arbitrary"),
                                             vmem_limit_bytes=VMEM_LIMIT),
    )(page_table.reshape(-1), q_bf.reshape(b, 1, SB_W), bias_rep, _suffix_matrix(),
      *([ckt] * npg), *([cvt] * npg))
    return out.reshape(b, SB_W)


def _hgrn_sample_kernel(lbl_ref, s_ref, q_ref, f_ref, v_ref, o_ref, sn_ref):
    l0 = lbl_ref[0]
    l1 = lbl_ref[1]
    m = jnp.maximum(l0, l1)
    e0 = jnp.exp(l0 - m)
    e1 = jnp.exp(l1 - m)
    lb = e0 / (e0 + e1)
    for r in range(s_ref.shape[0]):
        f = lb + (1.0 - lb) * jax.nn.sigmoid(f_ref[r])
        k = 1.0 - f
        v = v_ref[r]
        v_rows = jnp.concatenate([jnp.broadcast_to(v[:, h * HG_D:(h + 1) * HG_D], (HG_D, HG_D))
                                  for h in range(HG_HEADS)], axis=0)
        s_new = f * s_ref[r] + k * v_rows
        sn_ref[r] = s_new
        qs = q_ref[r] * s_new
        o_ref[r] = jnp.sum(qs.reshape(HG_HEADS, HG_D, HG_D), axis=1)


def _hgrn_sample(state, q_col, f_col, v_row, lb_logits):
    b = state.shape[0]
    rows = SAMPLE_ROWS_PER_STEP if b % SAMPLE_ROWS_PER_STEP == 0 else 1
    row = lambda shape: pl.BlockSpec((rows,) + shape, lambda bi: (bi, 0, 0))
    return pl.pallas_call(
        _hgrn_sample_kernel,
        grid=(b // rows,),
        in_specs=[pl.BlockSpec((2, HG_W, 1), lambda bi: (0, 0, 0)),
                  row((HG_W, HG_D)), row((HG_W, 1)), row((HG_W, 1)), row((1, HG_W))],
        out_specs=[row((HG_HEADS, HG_D)), row((HG_W, HG_D))],
        out_shape=[jax.ShapeDtypeStruct((b, HG_HEADS, HG_D), F32), jax.ShapeDtypeStruct((b, HG_W, HG_D), F32)],
        compiler_params=pltpu.CompilerParams(dimension_semantics=("arbitrary",)),
    )(lb_logits.reshape(2, HG_W, 1), state, q_col, f_col, v_row)


def _in_proj_outs(transposed_kv):
    qs = (LOG2E if transposed_kv else 1.0) / math.sqrt(SB_DH)
    q_out = (0, SB_W, BF16, qs)
    rest_out = (3 * SB_W, 3 * SB_W + D_REST, F32, 1.0)
    if transposed_kv:
        return [q_out, rest_out]
    return [q_out, (SB_W, 2 * SB_W, F32, 1.0), (2 * SB_W, 3 * SB_W, F32, 1.0), rest_out]


def kernel(x_prompt, x_sample, mem_prompt, cache_k, cache_v, page_table, state_hgrn, cache_mem_k, cache_mem_v,
           norm_gain, w_in, sb_bias, hg_lb_logits, hg_norm_gain, mem_norm_gain, w_mem_kv, w_out, final_norm_gain):
    batch, seq, d = x_prompt.shape
    dec_b = x_sample.shape[0]
    n_mem = mem_prompt.shape[1]
    depth = w_in.shape[0]
    assert depth == 1 and x_sample.shape[1] == 1
    l = 0
    lb_logits = hg_lb_logits.astype(F32)
    assert lb_logits.shape[0] == 2
    w_in_bf = w_in[l].astype(BF16)
    w_out_bf = w_out[l].astype(BF16)
    w_mem_bf = w_mem_kv[l].astype(BF16)
    bias = sb_bias[l].astype(F32)

    w_kv_t = w_in[l][:, SB_W:3 * SB_W].T.astype(BF16)
    q_bf, rest, kt_p, vt_p = _norm_proj(x_prompt, norm_gain[l], w_in_bf, _in_proj_outs(True), tm=256,
                                        wt_bf=w_kv_t, t_widths=(SB_W, SB_W))
    mkt, mvt = _norm_proj(mem_prompt, mem_norm_gain[l], w_mem_bf, [], tm=256,
                          wt_bf=w_mem_kv[l].T.astype(BF16), t_widths=(XA_W, XA_W))
    qs_bf, k_s, v_s, rest_s = _norm_proj(x_sample.reshape(1, dec_b, d), norm_gain[l], w_in_bf,
                                         _in_proj_outs(False), tm=dec_b)

    n_pool, page = cache_k.shape[1], cache_k.shape[2]
    ckt = jnp.transpose(cache_k[l], (0, 2, 3, 1)).reshape(n_pool, SB_W, page)
    cvt = jnp.transpose(cache_v[l], (0, 2, 3, 1)).reshape(n_pool, SB_W, page)
    n_pages = page_table.shape[1]
    fused = (n_pages % PAGES_PER_STEP == 0 and
             dec_b * (n_pages // PAGES_PER_STEP) == batch * (SB_W // LANES) * (seq // SB_TQ))
    if fused:
        sb_o, sb_o_s = _sb_prompt(q_bf, kt_p, vt_p, bias, sample=(qs_bf, bias, page_table, ckt, cvt))
    else:
        sb_o, _ = _sb_prompt(q_bf, kt_p, vt_p, bias)
        sb_o_s = _sb_sample(qs_bf, bias, page_table, ckt, cvt)

    rest3 = rest.reshape(batch, seq, D_REST)
    hg_o, st_t = _hgrn_prompt(rest3, lb_logits)
    y_prompt = _mix_out(x_prompt, sb_o.reshape(batch, seq, SB_W), hg_o, rest3, mkt, mvt,
                        w_out_bf, hg_norm_gain[l], final_norm_gain, tm=512)
    hgrn_prompt = jnp.stack([st_t[:, h * HG_D:(h + 1) * HG_D, h * HG_D:(h + 1) * HG_D]
                             for h in range(HG_HEADS)], axis=1).swapaxes(-1, -2)

    hq_s = rest_s[:, SB_W:SB_W + HG_W].reshape(dec_b, HG_W, 1)
    hf_s = rest_s[:, SB_W + HG_W:SB_W + 2 * HG_W].reshape(dec_b, HG_W, 1)
    hv_s = rest_s[:, SB_W + 2 * HG_W:SB_W + 3 * HG_W].reshape(dec_b, 1, HG_W)
    hg_o_s, st_s = _hgrn_sample(state_hgrn[l].astype(F32).reshape(dec_b, HG_W, HG_D), hq_s, hf_s, hv_s, lb_logits)
    y_sample = _mix_out(x_sample, sb_o_s.reshape(dec_b, 1, SB_W), hg_o_s.reshape(dec_b, 1, HG_W),
                        rest_s.reshape(dec_b, 1, D_REST),
                        jnp.transpose(cache_mem_k[l], (0, 2, 3, 1)).reshape(dec_b, XA_W, n_mem),
                        jnp.transpose(cache_mem_v[l], (0, 2, 3, 1)).reshape(dec_b, XA_W, n_mem),
                        w_out_bf, hg_norm_gain[l], final_norm_gain, tm=1)

    def token_major(a_t, heads, dh):
        b_, _, t_ = a_t.shape
        return jnp.transpose(a_t.reshape(b_, heads, dh, t_), (0, 3, 1, 2))[None]

    return (y_prompt, y_sample, token_major(kt_p, SB_HEADS, SB_DH), token_major(vt_p, SB_HEADS, SB_DH),
            hgrn_prompt.reshape(1, batch, HG_HEADS, HG_D, HG_D),
            token_major(mkt, XA_HEADS, XA_DH), token_major(mvt, XA_HEADS, XA_DH),
            k_s.reshape(1, dec_b, 1, SB_HEADS, SB_DH), v_s.reshape(1, dec_b, 1, SB_HEADS, SB_DH),
            st_s.reshape(1, dec_b, HG_HEADS, HG_D, HG_D).astype(state_hgrn.dtype))
```
